```python
import math
import jax, jax.numpy as jnp
from jax import lax
import numpy as np

D_MODEL = 1024
BATCH = 8
SEQ = 4096
DEPTH = 4

N_A_LAYERS = DEPTH // 2
N_B_LAYERS = DEPTH - N_A_LAYERS
SSM_GROUP = 16
N_GROUPS = D_MODEL // SSM_GROUP
SSM_STATE = 64
DT_MIN = 1e-3
DT_MAX = 1e-1
N_HEADS = 16
HEAD_DIM = D_MODEL // N_HEADS
Q_BLOCK = 128
ATTN_SCALE = HEAD_DIM ** -0.5
D_FF = ((8 * D_MODEL // 3 + 127) // 128) * 128
CONV_W = 3
EPS = 1e-6

kernel_name = "yoco_s5_fox_convffn_trunk"


def rmsnorm(x, g):
    xf = x.astype(jnp.float32)
    y = xf * lax.rsqrt(jnp.mean(xf * xf, axis=-1, keepdims=True) + EPS) * g.astype(jnp.float32)
    return y.astype(x.dtype)


def causal_dwconv(h, w, b):
    L = h.shape[1]
    hp = jnp.pad(h, ((0, 0), (CONV_W - 1, 0), (0, 0)))
    y = b
    for k in range(CONV_W):
        y = y + hp[:, k:k + L, :] * w[k]
    return y


def conv_ffn(h, w_in, conv_w, conv_b, w_out):
    u = causal_dwconv(h @ w_in, conv_w, conv_b)
    gate, up = jnp.split(u, 2, axis=-1)
    return (jax.nn.silu(gate) * up) @ w_out


def _ssm_combine(e_i, e_j):
    ai_re, ai_im, bi_re, bi_im = e_i
    aj_re, aj_im, bj_re, bj_im = e_j
    a_re = aj_re * ai_re - aj_im * ai_im
    a_im = aj_re * ai_im + aj_im * ai_re
    b_re = aj_re * bi_re - aj_im * bi_im + bj_re
    b_im = aj_re * bi_im + aj_im * bi_re + bj_im
    return (a_re, a_im, b_re, b_im)


def s5_mixer(h, lam_re, lam_im, log_dt, b_re, b_im, c_re, c_im, d_skip, w_glu):
    dtype = h.dtype
    bsz, L, _ = h.shape
    f32 = jnp.float32
    u = h.astype(f32).reshape(bsz, L, N_GROUPS, SSM_GROUP)
    lr = lam_re.astype(f32)
    li = lam_im.astype(f32)
    dt = jnp.exp(log_dt.astype(f32))[:, None]
    mag = jnp.exp(lr * dt)
    lb_re = mag * jnp.cos(li * dt)
    lb_im = mag * jnp.sin(li * dt)
    den = lr * lr + li * li
    nr = lb_re - 1.0
    fr = ((nr * lr + lb_im * li) / den)[..., None]
    fi = ((lb_im * lr - nr * li) / den)[..., None]
    br = b_re.astype(f32)
    bi = b_im.astype(f32)
    bb_re = fr * br - fi * bi
    bb_im = fr * bi + fi * br
    bu_re = jnp.einsum('blgh,gph->blgp', u, bb_re)
    bu_im = jnp.einsum('blgh,gph->blgp', u, bb_im)
    a_re = jnp.broadcast_to(lb_re[None, None], (1, L, N_GROUPS, SSM_STATE))
    a_im = jnp.broadcast_to(lb_im[None, None], (1, L, N_GROUPS, SSM_STATE))
    _, _, s_re, s_im = lax.associative_scan(_ssm_combine, (a_re, a_im, bu_re, bu_im), axis=1)
    y = (jnp.einsum('blgp,ghp->blgh', s_re, c_re.astype(f32))
         - jnp.einsum('blgp,ghp->blgh', s_im, c_im.astype(f32)))
    y = y.reshape(bsz, L, D_MODEL) + d_skip.astype(f32) * u.reshape(bsz, L, D_MODEL)
    y = jax.nn.gelu(y)
    z_a, z_g = jnp.split(y @ w_glu.astype(f32), 2, axis=-1)
    return (z_a * jax.nn.sigmoid(z_g)).astype(dtype)


def fox_shared_kv(h_kv, w_kvf, b_f):
    bsz, L, _ = h_kv.shape
    z = h_kv @ w_kvf
    k = z[..., :D_MODEL].reshape(bsz, L, N_HEADS, HEAD_DIM)
    v = z[..., D_MODEL:2 * D_MODEL].reshape(bsz, L, N_HEADS, HEAD_DIM)
    f_logit = z[..., 2 * D_MODEL:].astype(jnp.float32) + b_f.astype(jnp.float32)
    cum = jnp.cumsum(jax.nn.log_sigmoid(f_logit), axis=1)
    return k, v, cum


def fox_attention(h, w_q, w_o, k, v, cum):
    dtype = h.dtype
    bsz, L, _ = h.shape
    nb = L // Q_BLOCK
    f32 = jnp.float32
    q = (h @ w_q).reshape(bsz, nb, Q_BLOCK, N_HEADS, HEAD_DIM).transpose(1, 0, 2, 3, 4)
    cq = cum.reshape(bsz, nb, Q_BLOCK, N_HEADS).transpose(1, 0, 2, 3)
    kf = k.astype(f32)
    vf = v.astype(f32)
    ck = cum.transpose(0, 2, 1)[:, :, None, :]
    kpos = jnp.arange(L, dtype=jnp.int32)
    starts = jnp.arange(nb, dtype=jnp.int32) * Q_BLOCK

    def one_block(args):
        qb, cqb, start = args
        s = jnp.einsum('bqhd,bkhd->bhqk', qb.astype(f32), kf) * ATTN_SCALE
        s = s + cqb.transpose(0, 2, 1)[..., None] - ck
        qpos = start + jnp.arange(Q_BLOCK, dtype=jnp.int32)
        mask = kpos[None, :] <= qpos[:, None]
        s = jnp.where(mask, s, -jnp.inf)
        p = jax.nn.softmax(s, axis=-1)
        return jnp.einsum('bhqk,bkhd->bqhd', p, vf)

    o = lax.map(one_block, (q, cq, starts))
    o = o.transpose(1, 0, 2, 3, 4).reshape(bsz, L, D_MODEL)
    return o.astype(dtype) @ w_o


def setup_inputs(seed: int = 0) -> dict:
    key = jax.random.key(seed)
    ks = jax.random.split(key, 24)
    nrm = jax.random.normal
    D, G, P, H, F = D_MODEL, N_GROUPS, SSM_STATE, SSM_GROUP, D_FF
    x = nrm(ks[0], (BATCH, SEQ, D), jnp.float32)
    g_mix = 1.0 + 0.02 * nrm(ks[1], (DEPTH, D), jnp.float32)
    g_ffn = 1.0 + 0.02 * nrm(ks[2], (DEPTH, D), jnp.float32)
    lam_re = -0.5 + 0.01 * nrm(ks[3], (N_A_LAYERS, G, P), jnp.float32)
    lam_im = (math.pi * jnp.arange(P, dtype=jnp.float32))[None, None, :] + 0.01 * nrm(ks[4], (N_A_LAYERS, G, P), jnp.float32)
    log_dt = jax.random.uniform(ks[5], (N_A_LAYERS, G), jnp.float32, math.log(DT_MIN), math.log(DT_MAX))
    ssm_b_re = nrm(ks[6], (N_A_LAYERS, G, P, H), jnp.float32) * (2 * H) ** -0.5
    ssm_b_im = nrm(ks[7], (N_A_LAYERS, G, P, H), jnp.float32) * (2 * H) ** -0.5
    ssm_c_re = nrm(ks[8], (N_A_LAYERS, G, H, P), jnp.float32) * P ** -0.5
    ssm_c_im = nrm(ks[9], (N_A_LAYERS, G, H, P), jnp.float32) * P ** -0.5
    ssm_d = nrm(ks[10], (N_A_LAYERS, D), jnp.float32)
    w_glu = nrm(ks[11], (N_A_LAYERS, D, 2 * D), jnp.float32) * D ** -0.5
    g_kv = 1.0 + 0.02 * nrm(ks[12], (D,), jnp.float32)
    w_kvf = nrm(ks[13], (D, 2 * D + N_HEADS), jnp.float32) * D ** -0.5
    b_f = 2.0 + 0.5 * nrm(ks[14], (N_HEADS,), jnp.float32)
    w_q = nrm(ks[15], (N_B_LAYERS, D, D), jnp.float32) * D ** -0.5
    w_o = nrm(ks[16], (N_B_LAYERS, D, D), jnp.float32) * D ** -0.5
    w_ffn_in = nrm(ks[17], (DEPTH, D, 2 * F), jnp.float32) * D ** -0.5
    ffn_conv_w = nrm(ks[18], (DEPTH, CONV_W, 2 * F), jnp.float32) * CONV_W ** -0.5
    ffn_conv_b = 0.01 * nrm(ks[19], (DEPTH, 2 * F), jnp.float32)
    w_ffn_out = nrm(ks[20], (DEPTH, F, D), jnp.float32) * F ** -0.5
    g_final = 1.0 + 0.02 * nrm(ks[21], (D,), jnp.float32)
    return {"x": x, "g_mix": g_mix, "g_ffn": g_ffn, "lam_re": lam_re, "lam_im": lam_im,
            "log_dt": log_dt, "ssm_b_re": ssm_b_re, "ssm_b_im": ssm_b_im, "ssm_c_re": ssm_c_re,
            "ssm_c_im": ssm_c_im, "ssm_d": ssm_d, "w_glu": w_glu, "g_kv": g_kv, "w_kvf": w_kvf,
            "b_f": b_f, "w_q": w_q, "w_o": w_o, "w_ffn_in": w_ffn_in, "ffn_conv_w": ffn_conv_w,
            "ffn_conv_b": ffn_conv_b, "w_ffn_out": w_ffn_out, "g_final": g_final}


def reference(x, g_mix, g_ffn, lam_re, lam_im, log_dt, ssm_b_re, ssm_b_im, ssm_c_re, ssm_c_im,
              ssm_d, w_glu, g_kv, w_kvf, b_f, w_q, w_o, w_ffn_in, ffn_conv_w, ffn_conv_b,
              w_ffn_out, g_final):
    h = x
    k = v = cum = None
    for layer in range(DEPTH):
        if layer < N_A_LAYERS:
            h = h + s5_mixer(rmsnorm(h, g_mix[layer]), lam_re[layer], lam_im[layer], log_dt[layer],
                             ssm_b_re[layer], ssm_b_im[layer], ssm_c_re[layer], ssm_c_im[layer],
                             ssm_d[layer], w_glu[layer])
        else:
            if layer == N_A_LAYERS:
                k, v, cum = fox_shared_kv(rmsnorm(h, g_kv), w_kvf, b_f)
            j = layer - N_A_LAYERS
            h = h + fox_attention(rmsnorm(h, g_mix[layer]), w_q[j], w_o[j], k, v, cum)
        h = h + conv_ffn(rmsnorm(h, g_ffn[layer]), w_ffn_in[layer], ffn_conv_w[layer],
                         ffn_conv_b[layer], w_ffn_out[layer])
    return rmsnorm(h, g_final)
```

```python
import functools

import jax
import jax.numpy as jnp
from jax import lax
from jax.experimental import pallas as pl
from jax.experimental.pallas import tpu as pltpu

EPS = 1e-6
V7X_LANES = 128
V7X_SUBLANES = 8
V7X_VMEM_LIMIT_BYTES = 56 * 1024 * 1024
F32 = jnp.float32
BF16 = jnp.bfloat16
NEG_BIG = -1e30


def _params(n_axes=1):
    return pltpu.CompilerParams(dimension_semantics=("arbitrary",) * n_axes,
                                vmem_limit_bytes=V7X_VMEM_LIMIT_BYTES)


def _const_spec(shape):
    nd = len(shape)
    return pl.BlockSpec(shape, lambda *_: (0,) * nd, pipeline_mode=pl.Buffered(1))


def _rms(x, g):
    return x * lax.rsqrt(jnp.mean(x * x, axis=-1, keepdims=True) + EPS) * g


def _s5_prep_kernel(lr_ref, li_ref, ldt_ref, br_ref, bi_ref, lbr_ref, lbi_ref, bbr_ref, bbi_ref):
    lr = lr_ref[...]
    li = li_ref[...]
    dt = jnp.exp(ldt_ref[...])
    mag = jnp.exp(lr * dt)
    lb_re = mag * jnp.cos(li * dt)
    lb_im = mag * jnp.sin(li * dt)
    den = lr * lr + li * li
    nr = lb_re - 1.0
    fr = (nr * lr + lb_im * li) / den
    fi = (lb_im * lr - nr * li) / den
    br = br_ref[...]
    bi = bi_ref[...]
    lbr_ref[...] = lb_re
    lbi_ref[...] = lb_im
    bbr_ref[...] = fr * br - fi * bi
    bbi_ref[...] = fr * bi + fi * br


def _s5_prep(lam_re, lam_im, log_dt, b_re, b_im):
    g, p, h = b_re.shape
    rep = lambda a: jnp.repeat(a, h, axis=0)
    args = (rep(lam_re), rep(lam_im), rep(jnp.broadcast_to(log_dt[:, None], (g, p))),
            b_re.transpose(0, 2, 1).reshape(g * h, p), b_im.transpose(0, 2, 1).reshape(g * h, p))
    shp = jax.ShapeDtypeStruct((g * h, p), F32)
    lbr, lbi, bbr, bbi = pl.pallas_call(_s5_prep_kernel, out_shape=(shp,) * 4, name="s5_prep")(*args)
    lb_re = lbr.reshape(g, h, p)[:, 0, :]
    lb_im = lbi.reshape(g, h, p)[:, 0, :]
    return lb_re, lb_im, bbr.reshape(g, h, p), bbi.reshape(g, h, p)


def _block_diag_in(bb_re, bb_im):
    g, h, p = bb_re.shape
    gpb = V7X_LANES // h
    nb = g // gpb
    eye = jnp.eye(gpb, dtype=F32)
    bb = jnp.stack([bb_re, bb_im]).reshape(2, nb, gpb, h, p)
    return jnp.einsum("rkghp,gj->kghrjp", bb, eye).reshape(nb, gpb * h, 2 * gpb * p)


def _block_diag_out(c):
    g, h, p = c.shape
    gpb = V7X_LANES // h
    nb = g // gpb
    eye = jnp.eye(gpb, dtype=F32)
    return jnp.einsum("kghp,gj->kjpgh", c.reshape(nb, gpb, h, p), eye).reshape(nb, gpb * p, gpb * h)


def _s5_kernel(h_ref, g_ref, bblk_ref, cre_ref, cim_ref, are_ref, aim_ref, d_ref, wglu_ref,
               o_ref, bu_ref, st_ref, *, nb, bsz, steps, unroll):
    half = are_ref.shape[1] // nb
    width = 2 * half

    @pl.when(pl.program_id(0) == 0)
    def _():
        st_ref[...] = jnp.zeros_like(st_ref)

    x = h_ref[...]
    u = _rms(x, g_ref[...])
    ub = u.astype(BF16)
    for k in range(nb):
        bu_ref[:, k * width:(k + 1) * width] = jnp.dot(
            ub[:, k * V7X_LANES:(k + 1) * V7X_LANES], bblk_ref[k], preferred_element_type=F32)

    for k in range(nb):
        re_cols = slice(k * width, k * width + half)
        im_cols = slice(k * width + half, (k + 1) * width)
        a_re = jnp.broadcast_to(are_ref[:, k * half:(k + 1) * half], (bsz, half))
        a_im = jnp.broadcast_to(aim_ref[:, k * half:(k + 1) * half], (bsz, half))

        def step(t, carry, re_cols=re_cols, im_cols=im_cols, a_re=a_re, a_im=a_im):
            s_re, s_im = carry
            rows = pl.ds(pl.multiple_of(t * bsz, bsz), bsz)
            n_re = a_re * s_re - a_im * s_im + bu_ref[rows, re_cols]
            n_im = a_re * s_im + a_im * s_re + bu_ref[rows, im_cols]
            bu_ref[rows, re_cols] = n_re
            bu_ref[rows, im_cols] = n_im
            return n_re, n_im

        s_re, s_im = lax.fori_loop(0, steps, step, (st_ref[:, re_cols], st_ref[:, im_cols]),
                                   unroll=unroll)
        st_ref[:, re_cols] = s_re
        st_ref[:, im_cols] = s_im

    ys = []
    for k in range(nb):
        s_re = bu_ref[:, k * width:k * width + half].astype(BF16)
        s_im = bu_ref[:, k * width + half:(k + 1) * width].astype(BF16)
        ys.append(jnp.dot(s_re, cre_ref[k], preferred_element_type=F32)
                  - jnp.dot(s_im, cim_ref[k], preferred_element_type=F32))
    y = jnp.concatenate(ys, axis=-1) + d_ref[...] * u
    y = jax.nn.gelu(y)
    z = jnp.dot(y.astype(BF16), wglu_ref[...], preferred_element_type=F32)
    dm = x.shape[-1]
    o_ref[...] = x + z[:, :dm] * jax.nn.sigmoid(z[:, dm:])


def _s5_layer(h, g, bblk, cre, cim, a_re, a_im, d_skip, w_glu, *, bsz, steps):
    rows_total, dm = h.shape
    rows = bsz * steps
    nb = bblk.shape[0]
    nstate = a_re.shape[1]
    kern = functools.partial(_s5_kernel, nb=nb, bsz=bsz, steps=steps, unroll=4)
    return pl.pallas_call(
        kern,
        out_shape=jax.ShapeDtypeStruct((rows_total, dm), F32),
        grid=(rows_total // rows,),
        in_specs=[pl.BlockSpec((rows, dm), lambda i: (i, 0)),
                  _const_spec((1, dm)), _const_spec(bblk.shape), _const_spec(cre.shape),
                  _const_spec(cim.shape), _const_spec(a_re.shape), _const_spec(a_im.shape),
                  _const_spec((1, dm)), _const_spec(w_glu.shape)],
        out_specs=pl.BlockSpec((rows, dm), lambda i: (i, 0)),
        scratch_shapes=[pltpu.VMEM((rows, 2 * nstate), F32), pltpu.VMEM((bsz, 2 * nstate), F32)],
        compiler_params=_params(), name="s5_layer",
    )(h, g, bblk, cre, cim, a_re, a_im, d_skip, w_glu)


def _ffn_kernel(h_ref, g_ref, win_ref, cw_ref, cb_ref, wout_ref, o_ref, ubuf_ref, halo_ref, act_ref,
                *, shift, halo, seq_tiles, fc):
    tm = h_ref.shape[0]
    nchunk = act_ref.shape[1] // fc
    x = h_ref[...]
    hb = _rms(x, g_ref[...]).astype(BF16)

    @pl.when(pl.program_id(0) % seq_tiles == 0)
    def _():
        halo_ref[...] = jnp.zeros_like(halo_ref)

    for c in range(nchunk):
        cols = slice(c * 2 * fc, (c + 1) * 2 * fc)
        uc = jnp.dot(hb, win_ref[:, cols], preferred_element_type=F32)
        ubuf_ref[0:halo, :] = halo_ref[:, cols]
        ubuf_ref[halo:halo + tm, :] = uc
        halo_ref[:, cols] = uc[tm - halo:, :]
        y = (cb_ref[:, cols] + cw_ref[2:3, cols] * uc
             + cw_ref[1:2, cols] * ubuf_ref[halo - shift:halo - shift + tm, :]
             + cw_ref[0:1, cols] * ubuf_ref[halo - 2 * shift:halo - 2 * shift + tm, :])
        act_ref[:, c * fc:(c + 1) * fc] = (jax.nn.silu(y[:, :fc]) * y[:, fc:]).astype(BF16)

    o_ref[...] = x + jnp.dot(act_ref[...], wout_ref[...], preferred_element_type=F32)


def _ffn_layer(h, g, w_in, conv_w, conv_b, w_out, *, tm, shift, seq_tiles, fc):
    rows_total, dm = h.shape
    ff = w_out.shape[0]
    halo = max(V7X_SUBLANES, 2 * shift)
    kern = functools.partial(_ffn_kernel, shift=shift, halo=halo, seq_tiles=seq_tiles, fc=fc)
    return pl.pallas_call(
        kern,
        out_shape=jax.ShapeDtypeStruct((rows_total, dm), F32),
        grid=(rows_total // tm,),
        in_specs=[pl.BlockSpec((tm, dm), lambda i: (i, 0)),
                  _const_spec((1, dm)), _const_spec(w_in.shape), _const_spec(conv_w.shape),
                  _const_spec(conv_b.shape), _const_spec(w_out.shape)],
        out_specs=pl.BlockSpec((tm, dm), lambda i: (i, 0)),
        scratch_shapes=[pltpu.VMEM((tm + halo, 2 * fc), F32), pltpu.VMEM((halo, 2 * ff), F32),
                        pltpu.VMEM((tm, ff), BF16)],
        compiler_params=_params(), name="conv_ffn",
    )(h, g, w_in, conv_w, conv_b, w_out)


def _interleave_gate_up(a, ff, fc):
    lead = a.shape[:-1]
    return a.reshape(*lead, 2, ff // fc, fc).swapaxes(-3, -2).reshape(*lead, 2 * ff)


def _norm_proj_kernel(h_ref, g_ref, w_ref, *o_refs, scale):
    hb = _rms(h_ref[...], g_ref[...]).astype(BF16)
    z = jnp.dot(hb, w_ref[...], preferred_element_type=F32)
    col = 0
    for o_ref in o_refs:
        n = o_ref.shape[1]
        o_ref[...] = (z[:, col:col + n] * scale).astype(o_ref.dtype)
        col += n


def _norm_proj(h, g, w, out_widths, out_dtypes, *, tm, scale=1.0):
    rows_total, dm = h.shape
    return pl.pallas_call(
        functools.partial(_norm_proj_kernel, scale=scale),
        out_shape=tuple(jax.ShapeDtypeStruct((rows_total, n), dt) for n, dt in zip(out_widths, out_dtypes)),
        grid=(rows_total // tm,),
        in_specs=[pl.BlockSpec((tm, dm), lambda i: (i, 0)), _const_spec((1, dm)), _const_spec(w.shape)],
        out_specs=tuple(pl.BlockSpec((tm, n), lambda i: (i, 0)) for n in out_widths),
        compiler_params=_params(), name="norm_proj",
    )(h, g, w)


def _out_proj_kernel(h_ref, o_ref, w_ref, out_ref):
    out_ref[...] = h_ref[...] + jnp.dot(o_ref[...], w_ref[...], preferred_element_type=F32)


def _out_proj(h, o, w, *, tm):
    rows_total, dm = h.shape
    return pl.pallas_call(
        _out_proj_kernel,
        out_shape=jax.ShapeDtypeStruct((rows_total, dm), F32),
        grid=(rows_total // tm,),
        in_specs=[pl.BlockSpec((tm, dm), lambda i: (i, 0)), pl.BlockSpec((tm, dm), lambda i: (i, 0)),
                  _const_spec(w.shape)],
        out_specs=pl.BlockSpec((tm, dm), lambda i: (i, 0)),
        compiler_params=_params(), name="out_proj",
    )(h, o, w)


def _cumsum_kernel(f_ref, b_ref, o_ref):
    x = jax.nn.log_sigmoid(f_ref[...] + b_ref[...])
    n = x.shape[1]
    lane = lax.broadcasted_iota(jnp.int32, x.shape, 1)
    shift = 1
    while shift < n:
        x = x + jnp.where(lane >= shift, pltpu.roll(x, shift, axis=1), 0.0)
        shift *= 2
    o_ref[...] = x


def _forget_cumsum(f_t, b_col):
    return pl.pallas_call(_cumsum_kernel, out_shape=jax.ShapeDtypeStruct(f_t.shape, F32),
                          name="forget_cumsum")(f_t, b_col)


def _attn_kernel(q_ref, kt_ref, v_ref, cq_ref, ck_ref, o_ref, *, hd):
    tq = q_ref.shape[1]
    tk = kt_ref.shape[4]
    qi = pl.program_id(2)
    pr = pl.program_id(1)
    q = q_ref[0]
    first = lax.broadcasted_iota(jnp.int32, (tq, V7X_LANES), 1) < hd
    zero = jnp.zeros_like(q)
    qs = (jnp.where(first, q, zero), jnp.where(first, zero, q))
    cq_all = cq_ref[0]
    head_lane = lax.broadcasted_iota(jnp.int32, cq_all.shape, 1)
    cqs = tuple(jnp.sum(jnp.where(head_lane == 2 * pr + hh, cq_all, 0.0), axis=-1, keepdims=True)
                for hh in range(2))
    causal = (lax.broadcasted_iota(jnp.int32, (tq, tk), 1)
              <= lax.broadcasted_iota(jnp.int32, (tq, tk), 0))

    def block(j, carry, masked):
        m0, l0, m1, l1, acc = carry
        kt = kt_ref[0, 0, j]
        v = v_ref[0, pl.ds(pl.multiple_of(j * tk, tk), tk), :]
        ck = ck_ref[0, 0, j]
        stats = []
        for hh, (m, l) in enumerate(((m0, l0), (m1, l1))):
            s = jnp.dot(qs[hh], kt, preferred_element_type=F32) + (cqs[hh] - ck[hh:hh + 1, :])
            if masked:
                s = jnp.where(causal, s, -jnp.inf)
            m_new = jnp.maximum(m, jnp.max(s, axis=-1, keepdims=True))
            alpha = jnp.exp(m - m_new)
            p = jnp.exp(s - m_new)
            l_new = alpha * l + jnp.sum(p, axis=-1, keepdims=True)
            pv = jnp.dot(p.astype(BF16), v, preferred_element_type=F32)
            stats.append((m_new, l_new, alpha, pv))
        (m0, l0, a0, pv0), (m1, l1, a1, pv1) = stats
        acc = jnp.where(first, a0, a1) * acc + jnp.where(first, pv0, pv1)
        return m0, l0, m1, l1, acc

    col = lambda v: jnp.full((tq, 1), v, F32)
    init = (col(NEG_BIG), col(0.0), col(NEG_BIG), col(0.0), jnp.zeros((tq, V7X_LANES), F32))
    carry = lax.fori_loop(0, qi, functools.partial(block, masked=False), init)
    _, l0, _, l1, acc = block(qi, carry, masked=True)
    o_ref[0] = (acc / jnp.where(first, l0, l1)).astype(o_ref.dtype)


def _attention(q, kt, v, cq, ck, *, hd, tq):
    bsz, seq, dm = q.shape
    npair = dm // V7X_LANES
    nk, tk = kt.shape[2], kt.shape[4]
    nh = cq.shape[2]
    return pl.pallas_call(
        functools.partial(_attn_kernel, hd=hd),
        out_shape=jax.ShapeDtypeStruct((bsz, seq, dm), BF16),
        grid=(bsz, npair, seq // tq),
        in_specs=[pl.BlockSpec((1, tq, V7X_LANES), lambda b, p, i: (b, i, p)),
                  pl.BlockSpec((1, 1, nk, V7X_LANES, tk), lambda b, p, i: (b, p, 0, 0, 0)),
                  pl.BlockSpec((1, seq, V7X_LANES), lambda b, p, i: (b, 0, p)),
                  pl.BlockSpec((1, tq, nh), lambda b, p, i: (b, i, 0)),
                  pl.BlockSpec((1, 1, nk, 2, tk), lambda b, p, i: (b, p, 0, 0, 0))],
        out_specs=pl.BlockSpec((1, tq, V7X_LANES), lambda b, p, i: (b, i, p)),
        compiler_params=_params(3), name="fox_attention",
    )(q, kt, v, cq, ck)


def kernel(x, g_mix, g_ffn, lam_re, lam_im, log_dt, ssm_b_re, ssm_b_im, ssm_c_re, ssm_c_im, ssm_d,
           w_glu, g_kv, w_kvf, b_f, w_q, w_o, w_ffn_in, ffn_conv_w, ffn_conv_b, w_ffn_out, g_final):
    bsz, seq, dm = x.shape
    depth = g_mix.shape[0]
    n_a = lam_re.shape[0]
    nh = b_f.shape[0]
    hd = dm // nh
    ff = w_ffn_out.shape[1]
    assert bsz == V7X_SUBLANES and 2 * hd == V7X_LANES

    tm = min(512, seq)
    steps = tm // bsz
    fc = 256 if ff % 256 == 0 else V7X_LANES
    tq = min(256, seq)
    row = lambda a: a.reshape(1, -1)

    w_in_b = _interleave_gate_up(w_ffn_in, ff, fc).astype(BF16)
    conv_w_i = _interleave_gate_up(ffn_conv_w, ff, fc)
    conv_b_i = _interleave_gate_up(ffn_conv_b, ff, fc)
    w_out_b = w_ffn_out.astype(BF16)

    def ffn(h, layer, shift, seq_tiles):
        return _ffn_layer(h, row(g_ffn[layer]), w_in_b[layer], conv_w_i[layer], row(conv_b_i[layer]),
                          w_out_b[layer], tm=tm, shift=shift, seq_tiles=seq_tiles, fc=fc)

    h = x.transpose(1, 0, 2).reshape(seq * bsz, dm)
    for layer in range(n_a):
        lb_re, lb_im, bb_re, bb_im = _s5_prep(lam_re[layer], lam_im[layer], log_dt[layer],
                                              ssm_b_re[layer], ssm_b_im[layer])
        h = _s5_layer(h, row(g_mix[layer]), _block_diag_in(bb_re, bb_im).astype(BF16),
                      _block_diag_out(ssm_c_re[layer]).astype(BF16),
                      _block_diag_out(ssm_c_im[layer]).astype(BF16),
                      row(lb_re), row(lb_im), row(ssm_d[layer]), w_glu[layer].astype(BF16),
                      bsz=bsz, steps=steps)
        h = ffn(h, layer, bsz, seq * bsz // tm)

    h = h.reshape(seq, bsz, dm).transpose(1, 0, 2).reshape(bsz * seq, dm)
    fpad = V7X_LANES
    w_kvf_b = jnp.pad(w_kvf, ((0, 0), (0, fpad - nh))).astype(BF16)
    k, v, f_logit = _norm_proj(h, row(g_kv), w_kvf_b, (dm, dm, fpad), (BF16, BF16, F32), tm=tm)
    f_t = f_logit[:, :nh].reshape(bsz, seq, nh).transpose(0, 2, 1).reshape(bsz * nh, seq)
    cum_t = _forget_cumsum(f_t, jnp.tile(b_f, bsz).reshape(bsz * nh, 1)).reshape(bsz, nh, seq)
    nk = seq // tq
    cq = cum_t.transpose(0, 2, 1)
    ck = cum_t.reshape(bsz, nh // 2, 2, nk, tq).transpose(0, 1, 3, 2, 4)
    kt = k.reshape(bsz, nk, tq, nh // 2, V7X_LANES).transpose(0, 3, 1, 4, 2)
    v3 = v.reshape(bsz, seq, dm)
    for layer in range(n_a, depth):
        j = layer - n_a
        (q,) = _norm_proj(h, row(g_mix[layer]), w_q[j].astype(BF16), (dm,), (BF16,), tm=tm,
                          scale=hd ** -0.5)
        o = _attention(q.reshape(bsz, seq, dm), kt, v3, cq, ck, hd=hd, tq=tq)
        h = _out_proj(h, o.reshape(bsz * seq, dm), w_o[j].astype(BF16), tm=tm)
        h = ffn(h, layer, 1, seq // tm)

    return _final_norm(h, row(g_final), tm=tm).reshape(bsz, seq, dm)


def _final_norm_kernel(h_ref, g_ref, o_ref):
    o_ref[...] = _rms(h_ref[...], g_ref[...])


def _final_norm(h, g, *, tm):
    rows_total, dm = h.shape
    return pl.pallas_call(
        _final_norm_kernel,
        out_shape=jax.ShapeDtypeStruct((rows_total, dm), F32),
        grid=(rows_total // tm,),
        in_specs=[pl.BlockSpec((tm, dm), lambda i: (i, 0)), _const_spec((1, dm))],
        out_specs=pl.BlockSpec((tm, dm), lambda i: (i, 0)),
        compiler_params=_params(), name="final_norm",
    )(h, g)
```

```python
import functools

import jax
import jax.numpy as jnp
from jax import lax
from jax.experimental import pallas as pl
from jax.experimental.pallas import tpu as pltpu

EPS = 1e-6
V7X_LANES = 128
V7X_SUBLANES = 8
V7X_VMEM_LIMIT_BYTES = 56 * 1024 * 1024
F32 = jnp.float32
BF16 = jnp.bfloat16
NEG_BIG = -1e30


def _params(n_axes=1):
    return pltpu.CompilerParams(dimension_semantics=("arbitrary",) * n_axes,
                                vmem_limit_bytes=V7X_VMEM_LIMIT_BYTES)


def _const_spec(shape):
    nd = len(shape)
    return pl.BlockSpec(shape, lambda *_: (0,) * nd, pipeline_mode=pl.Buffered(1))


def _rms(x, g):
    return x * lax.rsqrt(jnp.mean(x * x, axis=-1, keepdims=True) + EPS) * g


def _s5_prep_kernel(lr_ref, li_ref, ldt_ref, br_ref, bi_ref, lbr_ref, lbi_ref, bbr_ref, bbi_ref):
    lr = lr_ref[...]
    li = li_ref[...]
    dt = jnp.exp(ldt_ref[...])
    mag = jnp.exp(lr * dt)
    lb_re = mag * jnp.cos(li * dt)
    lb_im = mag * jnp.sin(li * dt)
    den = lr * lr + li * li
    nr = lb_re - 1.0
    fr = (nr * lr + lb_im * li) / den
    fi = (lb_im * lr - nr * li) / den
    br = br_ref[...]
    bi = bi_ref[...]
    lbr_ref[...] = lb_re
    lbi_ref[...] = lb_im
    bbr_ref[...] = fr * br - fi * bi
    bbi_ref[...] = fr * bi + fi * br


def _s5_prep(lam_re, lam_im, log_dt, b_re, b_im):
    g, p, h = b_re.shape
    rep = lambda a: jnp.repeat(a, h, axis=0)
    args = (rep(lam_re), rep(lam_im), rep(jnp.broadcast_to(log_dt[:, None], (g, p))),
            b_re.transpose(0, 2, 1).reshape(g * h, p), b_im.transpose(0, 2, 1).reshape(g * h, p))
    shp = jax.ShapeDtypeStruct((g * h, p), F32)
    lbr, lbi, bbr, bbi = pl.pallas_call(_s5_prep_kernel, out_shape=(shp,) * 4, name="s5_prep")(*args)
    lb_re = lbr.reshape(g, h, p)[:, 0, :]
    lb_im = lbi.reshape(g, h, p)[:, 0, :]
    return lb_re, lb_im, bbr.reshape(g, h, p), bbi.reshape(g, h, p)


def _block_diag_in(bb_re, bb_im):
    g, h, p = bb_re.shape
    gpb = V7X_LANES // h
    nb = g // gpb
    eye = jnp.eye(gpb, dtype=F32)
    bb = jnp.stack([bb_re, bb_im]).reshape(2, nb, gpb, h, p)
    return jnp.einsum("rkghp,gj->kghrjp", bb, eye).reshape(nb, gpb * h, 2 * gpb * p)


def _block_diag_out(c):
    g, h, p = c.shape
    gpb = V7X_LANES // h
    nb = g // gpb
    eye = jnp.eye(gpb, dtype=F32)
    return jnp.einsum("kghp,gj->kjpgh", c.reshape(nb, gpb, h, p), eye).reshape(nb, gpb * p, gpb * h)


def _s5_kernel(h_ref, g_ref, bblk_ref, cre_ref, cim_ref, are_ref, aim_ref, d_ref, wglu_ref,
               o_ref, bu_ref, st_ref, *, nb, bsz, steps, unroll):
    half = are_ref.shape[1] // nb
    width = 2 * half

    @pl.when(pl.program_id(0) == 0)
    def _():
        st_ref[...] = jnp.zeros_like(st_ref)

    x = h_ref[...]
    u = _rms(x, g_ref[...])
    ub = u.astype(BF16)
    for k in range(nb):
        bu_ref[:, k * width:(k + 1) * width] = jnp.dot(
            ub[:, k * V7X_LANES:(k + 1) * V7X_LANES], bblk_ref[k], preferred_element_type=F32)

    for k in range(nb):
        re_cols = slice(k * width, k * width + half)
        im_cols = slice(k * width + half, (k + 1) * width)
        a_re = jnp.broadcast_to(are_ref[:, k * half:(k + 1) * half], (bsz, half))
        a_im = jnp.broadcast_to(aim_ref[:, k * half:(k + 1) * half], (bsz, half))

        def step(t, carry, re_cols=re_cols, im_cols=im_cols, a_re=a_re, a_im=a_im):
            s_re, s_im = carry
            rows = pl.ds(pl.multiple_of(t * bsz, bsz), bsz)
            n_re = a_re * s_re - a_im * s_im + bu_ref[rows, re_cols]
            n_im = a_re * s_im + a_im * s_re + bu_ref[rows, im_cols]
            bu_ref[rows, re_cols] = n_re
            bu_ref[rows, im_cols] = n_im
            return n_re, n_im

        s_re, s_im = lax.fori_loop(0, steps, step, (st_ref[:, re_cols], st_ref[:, im_cols]),
                                   unroll=unroll)
        st_ref[:, re_cols] = s_re
        st_ref[:, im_cols] = s_im

    ys = []
    for k in range(nb):
        s_re = bu_ref[:, k * width:k * width + half].astype(BF16)
        s_im = bu_ref[:, k * width + half:(k + 1) * width].astype(BF16)
        ys.append(jnp.dot(s_re, cre_ref[k], preferred_element_type=F32)
                  - jnp.dot(s_im, cim_ref[k], preferred_element_type=F32))
    y = jnp.concatenate(ys, axis=-1) + d_ref[...] * u
    y = jax.nn.gelu(y)
    z = jnp.dot(y.astype(BF16), wglu_ref[...], preferred_element_type=F32)
    dm = x.shape[-1]
    o_ref[...] = x + z[:, :dm] * jax.nn.sigmoid(z[:, dm:])


def _s5_layer(h, g, bblk, cre, cim, a_re, a_im, d_skip, w_glu, *, bsz, steps):
    rows_total, dm = h.shape
    rows = bsz * steps
    nb = bblk.shape[0]
    nstate = a_re.shape[1]
    kern = functools.partial(_s5_kernel, nb=nb, bsz=bsz, steps=steps, unroll=4)
    return pl.pallas_call(
        kern,
        out_shape=jax.ShapeDtypeStruct((rows_total, dm), F32),
        grid=(rows_total // rows,),
        in_specs=[pl.BlockSpec((rows, dm), lambda i: (i, 0)),
                  _const_spec((1, dm)), _const_spec(bblk.shape), _const_spec(cre.shape),
                  _const_spec(cim.shape), _const_spec(a_re.shape), _const_spec(a_im.shape),
                  _const_spec((1, dm)), _const_spec(w_glu.shape)],
        out_specs=pl.BlockSpec((rows, dm), lambda i: (i, 0)),
        scratch_shapes=[pltpu.VMEM((rows, 2 * nstate), F32), pltpu.VMEM((bsz, 2 * nstate), F32)],
        compiler_params=_params(), name="s5_layer",
    )(h, g, bblk, cre, cim, a_re, a_im, d_skip, w_glu)


def _ffn_kernel(h_ref, g_ref, win_ref, cw_ref, cb_ref, wout_ref, o_ref, ubuf_ref, halo_ref, act_ref,
                *, shift, halo, seq_tiles, fc):
    tm = h_ref.shape[0]
    nchunk = act_ref.shape[1] // fc
    x = h_ref[...]
    hb = _rms(x, g_ref[...]).astype(BF16)

    @pl.when(pl.program_id(0) % seq_tiles == 0)
    def _():
        halo_ref[...] = jnp.zeros_like(halo_ref)

    for c in range(nchunk):
        cols = slice(c * 2 * fc, (c + 1) * 2 * fc)
        uc = jnp.dot(hb, win_ref[:, cols], preferred_element_type=F32)
        ubuf_ref[0:halo, :] = halo_ref[:, cols]
        ubuf_ref[halo:halo + tm, :] = uc
        halo_ref[:, cols] = uc[tm - halo:, :]
        y = (cb_ref[:, cols] + cw_ref[2:3, cols] * uc
             + cw_ref[1:2, cols] * ubuf_ref[halo - shift:halo - shift + tm, :]
             + cw_ref[0:1, cols] * ubuf_ref[halo - 2 * shift:halo - 2 * shift + tm, :])
        act_ref[:, c * fc:(c + 1) * fc] = (jax.nn.silu(y[:, :fc]) * y[:, fc:]).astype(BF16)

    o_ref[...] = x + jnp.dot(act_ref[...], wout_ref[...], preferred_element_type=F32)


def _ffn_layer(h, g, w_in, conv_w, conv_b, w_out, *, tm, shift, seq_tiles, fc):
    rows_total, dm = h.shape
    ff = w_out.shape[0]
    halo = max(V7X_SUBLANES, 2 * shift)
    kern = functools.partial(_ffn_kernel, shift=shift, halo=halo, seq_tiles=seq_tiles, fc=fc)
    return pl.pallas_call(
        kern,
        out_shape=jax.ShapeDtypeStruct((rows_total, dm), F32),
        grid=(rows_total // tm,),
        in_specs=[pl.BlockSpec((tm, dm), lambda i: (i, 0)),
                  _const_spec((1, dm)), _const_spec(w_in.shape), _const_spec(conv_w.shape),
                  _const_spec(conv_b.shape), _const_spec(w_out.shape)],
        out_specs=pl.BlockSpec((tm, dm), lambda i: (i, 0)),
        scratch_shapes=[pltpu.VMEM((tm + halo, 2 * fc), F32), pltpu.VMEM((halo, 2 * ff), F32),
                        pltpu.VMEM((tm, ff), BF16)],
        compiler_params=_params(), name="conv_ffn",
    )(h, g, w_in, conv_w, conv_b, w_out)


def _interleave_gate_up(a, ff, fc):
    lead = a.shape[:-1]
    return a.reshape(*lead, 2, ff // fc, fc).swapaxes(-3, -2).reshape(*lead, 2 * ff)


def _norm_proj_kernel(h_ref, g_ref, w_ref, *o_refs, scale):
    hb = _rms(h_ref[...], g_ref[...]).astype(BF16)
    z = jnp.dot(hb, w_ref[...], preferred_element_type=F32)
    col = 0
    for o_ref in o_refs:
        n = o_ref.shape[1]
        o_ref[...] = (z[:, col:col + n] * scale).astype(o_ref.dtype)
        col += n


def _norm_proj(h, g, w, out_widths, out_dtypes, *, tm, scale=1.0):
    rows_total, dm = h.shape
    return pl.pallas_call(
        functools.partial(_norm_proj_kernel, scale=scale),
        out_shape=tuple(jax.ShapeDtypeStruct((rows_total, n), dt) for n, dt in zip(out_widths, out_dtypes)),
        grid=(rows_total // tm,),
        in_specs=[pl.BlockSpec((tm, dm), lambda i: (i, 0)), _const_spec((1, dm)), _const_spec(w.shape)],
        out_specs=tuple(pl.BlockSpec((tm, n), lambda i: (i, 0)) for n in out_widths),
        compiler_params=_params(), name="norm_proj",
    )(h, g, w)


def _out_proj_kernel(h_ref, o_ref, w_ref, out_ref):
    out_ref[...] = h_ref[...] + jnp.dot(o_ref[...], w_ref[...], preferred_element_type=F32)


def _out_proj(h, o, w, *, tm):
    rows_total, dm = h.shape
    return pl.pallas_call(
        _out_proj_kernel,
        out_shape=jax.ShapeDtypeStruct((rows_total, dm), F32),
        grid=(rows_total // tm,),
        in_specs=[pl.BlockSpec((tm, dm), lambda i: (i, 0)), pl.BlockSpec((tm, dm), lambda i: (i, 0)),
                  _const_spec(w.shape)],
        out_specs=pl.BlockSpec((tm, dm), lambda i: (i, 0)),
        compiler_params=_params(), name="out_proj",
    )(h, o, w)


LOG2E = 1.4426950408889634
N_SPLIT = 3


def _cumsum_kernel(f_ref, b_ref, c_ref, *part_refs):
    x = jax.nn.log_sigmoid(f_ref[...] + b_ref[...])
    n = x.shape[1]
    lane = lax.broadcasted_iota(jnp.int32, x.shape, 1)
    shift = 1
    while shift < n:
        x = x + jnp.where(lane >= shift, pltpu.roll(x, shift, axis=1), 0.0)
        shift *= 2
    c = x * LOG2E
    c_ref[...] = c
    rest = -c
    for ref in part_refs:
        piece = rest.astype(BF16)
        ref[...] = piece
        rest = rest - piece.astype(F32)


def _forget_cumsum(f_t, b_col):
    return pl.pallas_call(
        _cumsum_kernel,
        out_shape=(jax.ShapeDtypeStruct(f_t.shape, F32),) + (jax.ShapeDtypeStruct(f_t.shape, BF16),) * N_SPLIT,
        name="forget_cumsum")(f_t, b_col)


def _attn_kernel(qt_ref, k_ref, kaug_ref, vt_ref, cq_ref, o_ref, z0_ref, z1_ref, acc_ref, *, hd):
    tq = qt_ref.shape[3]
    tk = vt_ref.shape[4]
    qi = pl.program_id(2)
    qt = qt_ref[0, 0].astype(F32)
    row = lax.broadcasted_iota(jnp.int32, qt.shape, 0)
    ws = []
    for hh in range(2):
        qm = jnp.where((row >= hh * hd) & (row < (hh + 1) * hd), qt, 0.0)
        aw = jnp.where((row >= N_SPLIT * hh) & (row < N_SPLIT * (hh + 1)), 1.0, 0.0)
        ws.append(jnp.concatenate([qm, aw], axis=0).astype(BF16))
    cq = cq_ref[0, 0]
    keep = (lax.broadcasted_iota(jnp.int32, (tk, tq), 0)
            <= lax.broadcasted_iota(jnp.int32, (tk, tq), 1))

    def scores_into(z_ref, j):
        rows = pl.ds(pl.multiple_of(j * tk, tk), tk)
        kx = jnp.concatenate([k_ref[0, rows, :], kaug_ref[0, 0, rows, :]], axis=1)
        zmax = []
        for hh in range(2):
            z = jnp.dot(kx, ws[hh], preferred_element_type=F32)
            z_ref[hh] = z
            zmax.append(jnp.max(z, axis=0, keepdims=True))
        return tuple(zmax)

    def consume(z_ref, j, zmax, stats, masked):
        vt = vt_ref[0, 0, j]
        new_stats = []
        for hh in range(2):
            z, (m, l), c_row = z_ref[hh], stats[hh], cq[hh:hh + 1, :]
            if masked:
                z = jnp.where(keep, z, -jnp.inf)
                zm = jnp.max(z, axis=0, keepdims=True)
            else:
                zm = zmax[hh]
            m_new = jnp.maximum(m, zm + c_row)
            alpha = jnp.exp2(m - m_new)
            p = jnp.exp2(z - (m_new - c_row))
            new_stats.append((m_new, alpha * l + jnp.sum(p, axis=0, keepdims=True)))
            rows = slice(hh * hd, (hh + 1) * hd)
            acc_ref[rows, :] = alpha * acc_ref[rows, :] + jnp.dot(
                vt[rows, :], p.astype(BF16), preferred_element_type=F32)
        return tuple(new_stats)

    zbufs = (z0_ref, z1_ref)

    def skewed(i, zmax, stats, parity):
        return scores_into(zbufs[parity], i), consume(zbufs[1 - parity], i - 1, zmax, stats, False)

    def body(i, carry):
        return lax.cond(i % 2 == 0, functools.partial(skewed, parity=0),
                        functools.partial(skewed, parity=1), i, *carry)

    acc_ref[...] = jnp.zeros_like(acc_ref)
    stats = tuple((jnp.full((1, tq), NEG_BIG, F32), jnp.zeros((1, tq), F32)) for _ in range(2))
    zmax, stats = lax.fori_loop(1, qi + 1, body, (scores_into(z0_ref, 0), stats))
    stats = lax.cond(qi % 2 == 0,
                     lambda st: consume(z0_ref, qi, None, st, True),
                     lambda st: consume(z1_ref, qi, None, st, True), stats)
    for hh in range(2):
        rows = slice(hh * hd, (hh + 1) * hd)
        o_ref[0, 0, rows, :] = (acc_ref[rows, :] / stats[hh][1]).astype(o_ref.dtype)


def _attention(qt, k, kaug, vt, cq, *, hd):
    bsz, npair, _, seq = qt.shape
    nk, tq = vt.shape[2], vt.shape[4]
    return pl.pallas_call(
        functools.partial(_attn_kernel, hd=hd),
        out_shape=jax.ShapeDtypeStruct(qt.shape, BF16),
        grid=(bsz, npair, seq // tq),
        in_specs=[pl.BlockSpec((1, 1, V7X_LANES, tq), lambda b, p, i: (b, p, 0, i)),
                  pl.BlockSpec((1, seq, V7X_LANES), lambda b, p, i: (b, 0, p)),
                  pl.BlockSpec((1, 1, seq, V7X_LANES), lambda b, p, i: (b, p, 0, 0)),
                  pl.BlockSpec((1, 1, nk, V7X_LANES, tq), lambda b, p, i: (b, p, 0, 0, 0)),
                  pl.BlockSpec((1, 1, 2, tq), lambda b, p, i: (b, p, 0, i))],
        out_specs=pl.BlockSpec((1, 1, V7X_LANES, tq), lambda b, p, i: (b, p, 0, i)),
        scratch_shapes=[pltpu.VMEM((2, tq, tq), F32), pltpu.VMEM((2, tq, tq), F32),
                        pltpu.VMEM((V7X_LANES, tq), F32)],
        compiler_params=_params(3), name="fox_attention",
    )(qt, k, kaug, vt, cq)


def kernel(x, g_mix, g_ffn, lam_re, lam_im, log_dt, ssm_b_re, ssm_b_im, ssm_c_re, ssm_c_im, ssm_d,
           w_glu, g_kv, w_kvf, b_f, w_q, w_o, w_ffn_in, ffn_conv_w, ffn_conv_b, w_ffn_out, g_final):
    bsz, seq, dm = x.shape
    depth = g_mix.shape[0]
    n_a = lam_re.shape[0]
    nh = b_f.shape[0]
    hd = dm // nh
    ff = w_ffn_out.shape[1]
    assert bsz == V7X_SUBLANES and 2 * hd == V7X_LANES

    tm = min(512, seq)
    steps = tm // bsz
    fc = 256 if ff % 256 == 0 else V7X_LANES
    tq = min(512, seq)
    row = lambda a: a.reshape(1, -1)

    w_in_b = _interleave_gate_up(w_ffn_in, ff, fc).astype(BF16)
    conv_w_i = _interleave_gate_up(ffn_conv_w, ff, fc)
    conv_b_i = _interleave_gate_up(ffn_conv_b, ff, fc)
    w_out_b = w_ffn_out.astype(BF16)

    def ffn(h, layer, shift, seq_tiles):
        return _ffn_layer(h, row(g_ffn[layer]), w_in_b[layer], conv_w_i[layer], row(conv_b_i[layer]),
                          w_out_b[layer], tm=tm, shift=shift, seq_tiles=seq_tiles, fc=fc)

    h = x.transpose(1, 0, 2).reshape(seq * bsz, dm)
    for layer in range(n_a):
        lb_re, lb_im, bb_re, bb_im = _s5_prep(lam_re[layer], lam_im[layer], log_dt[layer],
                                              ssm_b_re[layer], ssm_b_im[layer])
        h = _s5_layer(h, row(g_mix[layer]), _block_diag_in(bb_re, bb_im).astype(BF16),
                      _block_diag_out(ssm_c_re[layer]).astype(BF16),
                      _block_diag_out(ssm_c_im[layer]).astype(BF16),
                      row(lb_re), row(lb_im), row(ssm_d[layer]), w_glu[layer].astype(BF16),
                      bsz=bsz, steps=steps)
        h = ffn(h, layer, bsz, seq * bsz // tm)

    h = h.reshape(seq, bsz, dm).transpose(1, 0, 2).reshape(bsz * seq, dm)
    fpad = V7X_LANES
    w_kvf_b = jnp.pad(w_kvf, ((0, 0), (0, fpad - nh))).astype(BF16)
    k, v, f_logit = _norm_proj(h, row(g_kv), w_kvf_b, (dm, dm, fpad), (BF16, BF16, F32), tm=tm)
    f_t = f_logit[:, :nh].reshape(bsz, seq, nh).transpose(0, 2, 1).reshape(bsz * nh, seq)
    cum_t, *parts = _forget_cumsum(f_t, jnp.tile(b_f, bsz).reshape(bsz * nh, 1))
    npair = nh // 2
    nk = seq // tq
    cq = cum_t.reshape(bsz, npair, 2, seq)
    kaug = jnp.stack(parts, axis=1).reshape(bsz, npair, 2 * N_SPLIT, seq).transpose(0, 1, 3, 2)
    kaug = jnp.pad(kaug, ((0, 0), (0, 0), (0, 0), (0, V7X_LANES - 2 * N_SPLIT)))
    k3 = k.reshape(bsz, seq, dm)
    vt = v.reshape(bsz, nk, tq, npair, V7X_LANES).transpose(0, 3, 1, 4, 2)
    for layer in range(n_a, depth):
        j = layer - n_a
        (q,) = _norm_proj(h, row(g_mix[layer]), w_q[j].astype(BF16), (dm,), (BF16,), tm=tm,
                          scale=hd ** -0.5 * LOG2E)
        qt = q.reshape(bsz, seq, npair, V7X_LANES).transpose(0, 2, 3, 1)
        ot = _attention(qt, k3, kaug, vt, cq, hd=hd)
        o = ot.transpose(0, 3, 1, 2).reshape(bsz * seq, dm)
        h = _out_proj(h, o, w_o[j].astype(BF16), tm=tm)
        h = ffn(h, layer, 1, seq // tm)

    return _final_norm(h, row(g_final), tm=tm).reshape(bsz, seq, dm)


def _final_norm_kernel(h_ref, g_ref, o_ref):
    o_ref[...] = _rms(h_ref[...], g_ref[...])


def _final_norm(h, g, *, tm):
    rows_total, dm = h.shape
    return pl.pallas_call(
        _final_norm_kernel,
        out_shape=jax.ShapeDtypeStruct((rows_total, dm), F32),
        grid=(rows_total // tm,),
        in_specs=[pl.BlockSpec((tm, dm), lambda i: (i, 0)), _const_spec((1, dm))],
        out_specs=pl.BlockSpec((tm, dm), lambda i: (i, 0)),
        compiler_params=_params(), name="final_norm",
    )(h, g)
```

```python
import functools

import jax
import jax.numpy as jnp
from jax import lax
from jax.experimental import pallas as pl
from jax.experimental.pallas import tpu as pltpu

EPS = 1e-6
V7X_LANES = 128
V7X_SUBLANES = 8
V7X_VMEM_LIMIT_BYTES = 56 * 1024 * 1024
F32 = jnp.float32
BF16 = jnp.bfloat16
NEG_BIG = -1e30


def _params(n_axes=1):
    return pltpu.CompilerParams(dimension_semantics=("arbitrary",) * n_axes,
                                vmem_limit_bytes=V7X_VMEM_LIMIT_BYTES)


def _const_spec(shape):
    nd = len(shape)
    return pl.BlockSpec(shape, lambda *_: (0,) * nd, pipeline_mode=pl.Buffered(1))


def _rms(x, g):
    return x * lax.rsqrt(jnp.mean(x * x, axis=-1, keepdims=True) + EPS) * g


def _s5_prep_kernel(lr_ref, li_ref, ldt_ref, br_ref, bi_ref, lbr_ref, lbi_ref, bbr_ref, bbi_ref):
    lr = lr_ref[...]
    li = li_ref[...]
    dt = jnp.exp(ldt_ref[...])
    mag = jnp.exp(lr * dt)
    lb_re = mag * jnp.cos(li * dt)
    lb_im = mag * jnp.sin(li * dt)
    den = lr * lr + li * li
    nr = lb_re - 1.0
    fr = (nr * lr + lb_im * li) / den
    fi = (lb_im * lr - nr * li) / den
    br = br_ref[...]
    bi = bi_ref[...]
    lbr_ref[...] = lb_re
    lbi_ref[...] = lb_im
    bbr_ref[...] = fr * br - fi * bi
    bbi_ref[...] = fr * bi + fi * br


def _s5_prep(lam_re, lam_im, log_dt, b_re, b_im):
    g, p, h = b_re.shape
    rep = lambda a: jnp.repeat(a, h, axis=0)
    args = (rep(lam_re), rep(lam_im), rep(jnp.broadcast_to(log_dt[:, None], (g, p))),
            b_re.transpose(0, 2, 1).reshape(g * h, p), b_im.transpose(0, 2, 1).reshape(g * h, p))
    shp = jax.ShapeDtypeStruct((g * h, p), F32)
    lbr, lbi, bbr, bbi = pl.pallas_call(_s5_prep_kernel, out_shape=(shp,) * 4, name="s5_prep")(*args)
    lb_re = lbr.reshape(g, h, p)[:, 0, :]
    lb_im = lbi.reshape(g, h, p)[:, 0, :]
    return lb_re, lb_im, bbr.reshape(g, h, p), bbi.reshape(g, h, p)


def _block_diag_in(bb_re, bb_im):
    g, h, p = bb_re.shape
    gpb = V7X_LANES // h
    nb = g // gpb
    eye = jnp.eye(gpb, dtype=F32)
    bb = jnp.stack([bb_re, bb_im]).reshape(2, nb, gpb, h, p)
    return jnp.einsum("rkghp,gj->kghrjp", bb, eye).reshape(nb, gpb * h, 2 * gpb * p)


def _block_diag_out(c):
    g, h, p = c.shape
    gpb = V7X_LANES // h
    nb = g // gpb
    eye = jnp.eye(gpb, dtype=F32)
    return jnp.einsum("kghp,gj->kjpgh", c.reshape(nb, gpb, h, p), eye).reshape(nb, gpb * p, gpb * h)


def _s5_kernel(h_ref, g_ref, bblk_ref, cre_ref, cim_ref, are_ref, aim_ref, d_ref, wglu_ref,
               o_ref, bu_ref, st_ref, *, nb, bsz, steps, unroll):
    half = are_ref.shape[1] // nb
    width = 2 * half

    @pl.when(pl.program_id(0) == 0)
    def _():
        st_ref[...] = jnp.zeros_like(st_ref)

    x = h_ref[...]
    u = _rms(x, g_ref[...])
    ub = u.astype(BF16)
    for k in range(nb):
        bu_ref[:, k * width:(k + 1) * width] = jnp.dot(
            ub[:, k * V7X_LANES:(k + 1) * V7X_LANES], bblk_ref[k], preferred_element_type=F32)

    for k in range(nb):
        re_cols = slice(k * width, k * width + half)
        im_cols = slice(k * width + half, (k + 1) * width)
        a_re = jnp.broadcast_to(are_ref[:, k * half:(k + 1) * half], (bsz, half))
        a_im = jnp.broadcast_to(aim_ref[:, k * half:(k + 1) * half], (bsz, half))

        def step(t, carry, re_cols=re_cols, im_cols=im_cols, a_re=a_re, a_im=a_im):
            s_re, s_im = carry
            rows = pl.ds(pl.multiple_of(t * bsz, bsz), bsz)
            n_re = a_re * s_re - a_im * s_im + bu_ref[rows, re_cols]
            n_im = a_re * s_im + a_im * s_re + bu_ref[rows, im_cols]
            bu_ref[rows, re_cols] = n_re
            bu_ref[rows, im_cols] = n_im
            return n_re, n_im

        s_re, s_im = lax.fori_loop(0, steps, step, (st_ref[:, re_cols], st_ref[:, im_cols]),
                                   unroll=unroll)
        st_ref[:, re_cols] = s_re
        st_ref[:, im_cols] = s_im

    ys = []
    for k in range(nb):
        s_re = bu_ref[:, k * width:k * width + half].astype(BF16)
        s_im = bu_ref[:, k * width + half:(k + 1) * width].astype(BF16)
        ys.append(jnp.dot(s_re, cre_ref[k], preferred_element_type=F32)
                  - jnp.dot(s_im, cim_ref[k], preferred_element_type=F32))
    y = jnp.concatenate(ys, axis=-1) + d_ref[...] * u
    y = jax.nn.gelu(y)
    z = jnp.dot(y.astype(BF16), wglu_ref[...], preferred_element_type=F32)
    dm = x.shape[-1]
    o_ref[...] = x + z[:, :dm] * jax.nn.sigmoid(z[:, dm:])


def _s5_layer(h, g, bblk, cre, cim, a_re, a_im, d_skip, w_glu, *, bsz, steps):
    rows_total, dm = h.shape
    rows = bsz * steps
    nb = bblk.shape[0]
    nstate = a_re.shape[1]
    kern = functools.partial(_s5_kernel, nb=nb, bsz=bsz, steps=steps, unroll=4)
    return pl.pallas_call(
        kern,
        out_shape=jax.ShapeDtypeStruct((rows_total, dm), F32),
        grid=(rows_total // rows,),
        in_specs=[pl.BlockSpec((rows, dm), lambda i: (i, 0)),
                  _const_spec((1, dm)), _const_spec(bblk.shape), _const_spec(cre.shape),
                  _const_spec(cim.shape), _const_spec(a_re.shape), _const_spec(a_im.shape),
                  _const_spec((1, dm)), _const_spec(w_glu.shape)],
        out_specs=pl.BlockSpec((rows, dm), lambda i: (i, 0)),
        scratch_shapes=[pltpu.VMEM((rows, 2 * nstate), F32), pltpu.VMEM((bsz, 2 * nstate), F32)],
        compiler_params=_params(), name="s5_layer",
    )(h, g, bblk, cre, cim, a_re, a_im, d_skip, w_glu)


def _ffn_kernel(*refs, shift, halo, seq_tiles, fc, pre_proj, post_norm):
    refs = list(refs)
    h_ref = refs.pop(0)
    if pre_proj:
        oin_ref, wo_ref = refs.pop(0), refs.pop(0)
    g_ref, win_ref, cw_ref, cb_ref, wout_ref = (refs.pop(0) for _ in range(5))
    if post_norm:
        gf_ref = refs.pop(0)
    o_ref, ubuf_ref, halo_ref, act_ref = refs
    tm = h_ref.shape[0]
    ff = act_ref.shape[1]
    x = h_ref[...]
    if pre_proj:
        x = x + jnp.dot(oin_ref[...], wo_ref[...], preferred_element_type=F32)
    hb = _rms(x, g_ref[...]).astype(BF16)

    @pl.when(pl.program_id(0) % seq_tiles == 0)
    def _():
        halo_ref[...] = jnp.zeros_like(halo_ref)

    def conv(u, cols, buf_cols):
        ubuf_ref[0:halo, buf_cols] = halo_ref[:, cols]
        ubuf_ref[halo:halo + tm, buf_cols] = u
        halo_ref[:, cols] = u[tm - halo:, :]
        return (cb_ref[:, cols] + cw_ref[2:3, cols] * u
                + cw_ref[1:2, cols] * ubuf_ref[halo - shift:halo - shift + tm, buf_cols]
                + cw_ref[0:1, cols] * ubuf_ref[halo - 2 * shift:halo - 2 * shift + tm, buf_cols])

    for c in range(ff // fc):
        gate_cols = slice(c * fc, (c + 1) * fc)
        up_cols = slice(ff + c * fc, ff + (c + 1) * fc)
        gate = conv(jnp.dot(hb, win_ref[:, gate_cols], preferred_element_type=F32), gate_cols, slice(0, fc))
        up = conv(jnp.dot(hb, win_ref[:, up_cols], preferred_element_type=F32), up_cols, slice(fc, 2 * fc))
        act_ref[:, gate_cols] = (jax.nn.silu(gate) * up).astype(BF16)

    out = x + jnp.dot(act_ref[...], wout_ref[...], preferred_element_type=F32)
    o_ref[...] = _rms(out, gf_ref[...]) if post_norm else out


def _ffn_layer(h, g, w_in, conv_w, conv_b, w_out, *, tm, shift, seq_tiles, fc, attn_out=None, w_o=None,
               g_final=None):
    rows_total, dm = h.shape
    ff = w_out.shape[0]
    halo = max(V7X_SUBLANES, 2 * shift)
    row_spec = pl.BlockSpec((tm, dm), lambda i: (i, 0))
    pre_proj, post_norm = attn_out is not None, g_final is not None
    args, specs = [h], [row_spec]
    if pre_proj:
        args += [attn_out, w_o]
        specs += [row_spec, _const_spec(w_o.shape)]
    args += [g, w_in, conv_w, conv_b, w_out]
    specs += [_const_spec((1, dm)), _const_spec(w_in.shape), _const_spec(conv_w.shape),
              _const_spec(conv_b.shape), _const_spec(w_out.shape)]
    if post_norm:
        args.append(g_final)
        specs.append(_const_spec((1, dm)))
    kern = functools.partial(_ffn_kernel, shift=shift, halo=halo, seq_tiles=seq_tiles, fc=fc,
                             pre_proj=pre_proj, post_norm=post_norm)
    return pl.pallas_call(
        kern,
        out_shape=jax.ShapeDtypeStruct((rows_total, dm), F32),
        grid=(rows_total // tm,),
        in_specs=specs,
        out_specs=row_spec,
        scratch_shapes=[pltpu.VMEM((tm + halo, 2 * fc), F32), pltpu.VMEM((halo, 2 * ff), F32),
                        pltpu.VMEM((tm, ff), BF16)],
        compiler_params=_params(), name="conv_ffn",
    )(*args)


def _norm_proj_kernel(h_ref, g_ref, w_ref, *o_refs, scale):
    hb = _rms(h_ref[...], g_ref[...]).astype(BF16)
    z = jnp.dot(hb, w_ref[...], preferred_element_type=F32)
    col = 0
    for o_ref in o_refs:
        n = o_ref.shape[1]
        o_ref[...] = (z[:, col:col + n] * scale).astype(o_ref.dtype)
        col += n


def _norm_proj(h, g, w, out_widths, out_dtypes, *, tm, scale=1.0):
    rows_total, dm = h.shape
    return pl.pallas_call(
        functools.partial(_norm_proj_kernel, scale=scale),
        out_shape=tuple(jax.ShapeDtypeStruct((rows_total, n), dt) for n, dt in zip(out_widths, out_dtypes)),
        grid=(rows_total // tm,),
        in_specs=[pl.BlockSpec((tm, dm), lambda i: (i, 0)), _const_spec((1, dm)), _const_spec(w.shape)],
        out_specs=tuple(pl.BlockSpec((tm, n), lambda i: (i, 0)) for n in out_widths),
        compiler_params=_params(), name="norm_proj",
    )(h, g, w)


LOG2E = 1.4426950408889634
N_SPLIT = 3


def _cumsum_kernel(f_ref, b_ref, c_ref, *part_refs):
    x = jax.nn.log_sigmoid(f_ref[...] + b_ref[...])
    n = x.shape[1]
    lane = lax.broadcasted_iota(jnp.int32, x.shape, 1)
    shift = 1
    while shift < n:
        x = x + jnp.where(lane >= shift, pltpu.roll(x, shift, axis=1), 0.0)
        shift *= 2
    c = x * LOG2E
    c_ref[...] = c
    rest = -c
    for ref in part_refs:
        piece = rest.astype(BF16)
        ref[...] = piece
        rest = rest - piece.astype(F32)


def _forget_cumsum(f_t, b_col):
    return pl.pallas_call(
        _cumsum_kernel,
        out_shape=(jax.ShapeDtypeStruct(f_t.shape, F32),) + (jax.ShapeDtypeStruct(f_t.shape, BF16),) * N_SPLIT,
        name="forget_cumsum")(f_t, b_col)


ONES_ROWS = 16


def _attn_kernel(q_ref, k_ref, kaug_ref, vt_ref, cq_ref, o_ref, z0_ref, z1_ref, acc_ref, *, hd):
    tq = q_ref.shape[1]
    tk = vt_ref.shape[4]
    qi = pl.program_id(2)
    qt = q_ref[0].astype(F32).T
    row = lax.broadcasted_iota(jnp.int32, qt.shape, 0)
    ws = []
    for hh in range(2):
        qm = jnp.where((row >= hh * hd) & (row < (hh + 1) * hd), qt, 0.0)
        aw = jnp.where((row >= N_SPLIT * hh) & (row < N_SPLIT * (hh + 1)), 1.0, 0.0)
        ws.append(jnp.concatenate([qm, aw], axis=0).astype(BF16))
    cq = cq_ref[0, 0]
    keep = (lax.broadcasted_iota(jnp.int32, (tk, tq), 0)
            <= lax.broadcasted_iota(jnp.int32, (tk, tq), 1))

    def scores_into(z_ref, j):
        rows = pl.ds(pl.multiple_of(j * tk, tk), tk)
        kx = jnp.concatenate([k_ref[0, rows, :], kaug_ref[0, 0, rows, :]], axis=1)
        zmax = []
        for hh in range(2):
            z = jnp.dot(kx, ws[hh], preferred_element_type=F32)
            z_ref[hh] = z
            zmax.append(jnp.max(z, axis=0, keepdims=True))
        return tuple(zmax)

    def consume(z_ref, j, zmax, stats, masked):
        vt = vt_ref[0, 0, j]
        ones = jnp.ones((ONES_ROWS, tk), BF16)
        new_stats = []
        for hh in range(2):
            z, m, c_row = z_ref[hh], stats[hh], cq[hh:hh + 1, :]
            if masked:
                z = jnp.where(keep, z, -jnp.inf)
                zm = jnp.max(z, axis=0, keepdims=True)
            else:
                zm = zmax[hh]
            m_new = jnp.maximum(m, zm + c_row)
            alpha = jnp.exp2(m - m_new)
            p = jnp.exp2(z - (m_new - c_row)).astype(BF16)
            new_stats.append(m_new)
            v_ext = jnp.concatenate([vt[hh * hd:(hh + 1) * hd, :], ones], axis=0)
            acc_ref[hh] = alpha * acc_ref[hh] + jnp.dot(v_ext, p, preferred_element_type=F32)
        return tuple(new_stats)

    zbufs = (z0_ref, z1_ref)

    def skewed(i, zmax, stats, parity):
        return scores_into(zbufs[parity], i), consume(zbufs[1 - parity], i - 1, zmax, stats, False)

    def body(i, carry):
        return lax.cond(i % 2 == 0, functools.partial(skewed, parity=0),
                        functools.partial(skewed, parity=1), i, *carry)

    acc_ref[...] = jnp.zeros_like(acc_ref)
    stats = tuple(jnp.full((1, tq), NEG_BIG, F32) for _ in range(2))
    zmax, stats = lax.fori_loop(1, qi + 1, body, (scores_into(z0_ref, 0), stats))
    lax.cond(qi % 2 == 0,
             lambda st: consume(z0_ref, qi, None, st, True),
             lambda st: consume(z1_ref, qi, None, st, True), stats)
    out_t = jnp.concatenate([acc_ref[hh, 0:hd, :] / acc_ref[hh, hd:hd + 1, :] for hh in range(2)], axis=0)
    o_ref[0] = out_t.T.astype(o_ref.dtype)


def _attention(q, k, kaug, vt, cq, *, hd):
    bsz, seq, dm = q.shape
    npair, nk, tq = vt.shape[1], vt.shape[2], vt.shape[4]
    pair_spec = pl.BlockSpec((1, tq, V7X_LANES), lambda b, p, i: (b, i, p))
    return pl.pallas_call(
        functools.partial(_attn_kernel, hd=hd),
        out_shape=jax.ShapeDtypeStruct(q.shape, BF16),
        grid=(bsz, npair, seq // tq),
        in_specs=[pair_spec,
                  pl.BlockSpec((1, seq, V7X_LANES), lambda b, p, i: (b, 0, p)),
                  pl.BlockSpec((1, 1, seq, V7X_LANES), lambda b, p, i: (b, p, 0, 0)),
                  pl.BlockSpec((1, 1, nk, V7X_LANES, tq), lambda b, p, i: (b, p, 0, 0, 0)),
                  pl.BlockSpec((1, 1, 2, tq), lambda b, p, i: (b, p, 0, i))],
        out_specs=pair_spec,
        scratch_shapes=[pltpu.VMEM((2, tq, tq), F32), pltpu.VMEM((2, tq, tq), F32),
                        pltpu.VMEM((2, hd + ONES_ROWS, tq), F32)],
        compiler_params=_params(3), name="fox_attention",
    )(q, k, kaug, vt, cq)


def kernel(x, g_mix, g_ffn, lam_re, lam_im, log_dt, ssm_b_re, ssm_b_im, ssm_c_re, ssm_c_im, ssm_d,
           w_glu, g_kv, w_kvf, b_f, w_q, w_o, w_ffn_in, ffn_conv_w, ffn_conv_b, w_ffn_out, g_final):
    bsz, seq, dm = x.shape
    depth = g_mix.shape[0]
    n_a = lam_re.shape[0]
    nh = b_f.shape[0]
    hd = dm // nh
    ff = w_ffn_out.shape[1]
    assert bsz == V7X_SUBLANES and 2 * hd == V7X_LANES

    tm = min(512, seq)
    steps = tm // bsz
    fc = 256 if ff % 256 == 0 else V7X_LANES
    tq = min(512, seq)
    row = lambda a: a.reshape(1, -1)

    w_in_b = w_ffn_in.astype(BF16)
    w_out_b = w_ffn_out.astype(BF16)

    def ffn(h, layer, shift, seq_tiles, **fused):
        return _ffn_layer(h, row(g_ffn[layer]), w_in_b[layer], ffn_conv_w[layer], row(ffn_conv_b[layer]),
                          w_out_b[layer], tm=tm, shift=shift, seq_tiles=seq_tiles, fc=fc, **fused)

    h = x.transpose(1, 0, 2).reshape(seq * bsz, dm)
    for layer in range(n_a):
        lb_re, lb_im, bb_re, bb_im = _s5_prep(lam_re[layer], lam_im[layer], log_dt[layer],
                                              ssm_b_re[layer], ssm_b_im[layer])
        h = _s5_layer(h, row(g_mix[layer]), _block_diag_in(bb_re, bb_im).astype(BF16),
                      _block_diag_out(ssm_c_re[layer]).astype(BF16),
                      _block_diag_out(ssm_c_im[layer]).astype(BF16),
                      row(lb_re), row(lb_im), row(ssm_d[layer]), w_glu[layer].astype(BF16),
                      bsz=bsz, steps=steps)
        h = ffn(h, layer, bsz, seq * bsz // tm)

    h = h.reshape(seq, bsz, dm).transpose(1, 0, 2).reshape(bsz * seq, dm)
    fpad = V7X_LANES
    w_kvf_b = jnp.pad(w_kvf, ((0, 0), (0, fpad - nh))).astype(BF16)
    k, v, f_logit = _norm_proj(h, row(g_kv), w_kvf_b, (dm, dm, fpad), (BF16, BF16, F32), tm=tm)
    f_t = f_logit[:, :nh].reshape(bsz, seq, nh).transpose(0, 2, 1).reshape(bsz * nh, seq)
    cum_t, *parts = _forget_cumsum(f_t, jnp.tile(b_f, bsz).reshape(bsz * nh, 1))
    npair = nh // 2
    nk = seq // tq
    cq = cum_t.reshape(bsz, npair, 2, seq)
    kaug = jnp.stack(parts, axis=1).reshape(bsz, npair, 2 * N_SPLIT, seq).transpose(0, 1, 3, 2)
    kaug = jnp.pad(kaug, ((0, 0), (0, 0), (0, 0), (0, V7X_LANES - 2 * N_SPLIT)))
    k3 = k.reshape(bsz, seq, dm)
    vt = v.reshape(bsz, nk, tq, npair, V7X_LANES).transpose(0, 3, 1, 4, 2)
    for layer in range(n_a, depth):
        j = layer - n_a
        (q,) = _norm_proj(h, row(g_mix[layer]), w_q[j].astype(BF16), (dm,), (BF16,), tm=tm,
                          scale=hd ** -0.5 * LOG2E)
        o = _attention(q.reshape(bsz, seq, dm), k3, kaug, vt, cq, hd=hd)
        h = ffn(h, layer, 1, seq // tm, attn_out=o.reshape(bsz * seq, dm), w_o=w_o[j].astype(BF16),
                g_final=row(g_final) if layer == depth - 1 else None)
    return h.reshape(bsz, seq, dm)
```

```python
import functools

import jax
import jax.numpy as jnp
from jax import lax
from jax.experimental import pallas as pl
from jax.experimental.pallas import tpu as pltpu

EPS = 1e-6
V7X_LANES = 128
V7X_SUBLANES = 8
V7X_VMEM_LIMIT_BYTES = 56 * 1024 * 1024
F32 = jnp.float32
BF16 = jnp.bfloat16
NEG_BIG = -1e30


def _params(n_axes=1):
    return pltpu.CompilerParams(dimension_semantics=("arbitrary",) * n_axes,
                                vmem_limit_bytes=V7X_VMEM_LIMIT_BYTES)


def _const_spec(shape):
    nd = len(shape)
    return pl.BlockSpec(shape, lambda *_: (0,) * nd, pipeline_mode=pl.Buffered(1))


def _rms(x, g):
    return x * lax.rsqrt(jnp.mean(x * x, axis=-1, keepdims=True) + EPS) * g


def _s5_prep_kernel(lr_ref, li_ref, ldt_ref, br_ref, bi_ref, lbr_ref, lbi_ref, bbr_ref, bbi_ref):
    lr = lr_ref[...]
    li = li_ref[...]
    dt = jnp.exp(ldt_ref[...])
    mag = jnp.exp(lr * dt)
    lb_re = mag * jnp.cos(li * dt)
    lb_im = mag * jnp.sin(li * dt)
    den = lr * lr + li * li
    nr = lb_re - 1.0
    fr = (nr * lr + lb_im * li) / den
    fi = (lb_im * lr - nr * li) / den
    br = br_ref[...]
    bi = bi_ref[...]
    lbr_ref[...] = lb_re
    lbi_ref[...] = lb_im
    bbr_ref[...] = fr * br - fi * bi
    bbi_ref[...] = fr * bi + fi * br


def _s5_prep(lam_re, lam_im, log_dt, b_re, b_im):
    g, p, h = b_re.shape
    rep = lambda a: jnp.repeat(a, h, axis=0)
    args = (rep(lam_re), rep(lam_im), rep(jnp.broadcast_to(log_dt[:, None], (g, p))),
            b_re.transpose(0, 2, 1).reshape(g * h, p), b_im.transpose(0, 2, 1).reshape(g * h, p))
    shp = jax.ShapeDtypeStruct((g * h, p), F32)
    lbr, lbi, bbr, bbi = pl.pallas_call(_s5_prep_kernel, out_shape=(shp,) * 4, name="s5_prep")(*args)
    lb_re = lbr.reshape(g, h, p)[:, 0, :]
    lb_im = lbi.reshape(g, h, p)[:, 0, :]
    return lb_re, lb_im, bbr.reshape(g, h, p), bbi.reshape(g, h, p)


def _block_diag_in(bb_re, bb_im):
    g, h, p = bb_re.shape
    gpb = V7X_LANES // h
    nb = g // gpb
    eye = jnp.eye(gpb, dtype=F32)
    bb = jnp.stack([bb_re, bb_im]).reshape(2, nb, gpb, h, p)
    return jnp.einsum("rkghp,gj->kghrjp", bb, eye).reshape(nb, gpb * h, 2 * gpb * p)


def _block_diag_out(c):
    g, h, p = c.shape
    gpb = V7X_LANES // h
    nb = g // gpb
    eye = jnp.eye(gpb, dtype=F32)
    return jnp.einsum("kghp,gj->kjpgh", c.reshape(nb, gpb, h, p), eye).reshape(nb, gpb * p, gpb * h)


def _s5_kernel(h_ref, g_ref, bblk_ref, cre_ref, cim_ref, are_ref, aim_ref, d_ref, wglu_ref,
               o_ref, bu_ref, st_ref, *, nb, bsz, steps, unroll):
    half = are_ref.shape[1] // nb
    width = 2 * half

    @pl.when(pl.program_id(0) == 0)
    def _():
        st_ref[...] = jnp.zeros_like(st_ref)

    x = h_ref[...]
    u = _rms(x, g_ref[...])
    ub = u.astype(BF16)
    for k in range(nb):
        bu_ref[:, k * width:(k + 1) * width] = jnp.dot(
            ub[:, k * V7X_LANES:(k + 1) * V7X_LANES], bblk_ref[k], preferred_element_type=F32)

    for k in range(nb):
        re_cols = slice(k * width, k * width + half)
        im_cols = slice(k * width + half, (k + 1) * width)
        a_re = jnp.broadcast_to(are_ref[:, k * half:(k + 1) * half], (bsz, half))
        a_im = jnp.broadcast_to(aim_ref[:, k * half:(k + 1) * half], (bsz, half))

        def step(t, carry, re_cols=re_cols, im_cols=im_cols, a_re=a_re, a_im=a_im):
            s_re, s_im = carry
            rows = pl.ds(pl.multiple_of(t * bsz, bsz), bsz)
            n_re = a_re * s_re - a_im * s_im + bu_ref[rows, re_cols]
            n_im = a_re * s_im + a_im * s_re + bu_ref[rows, im_cols]
            bu_ref[rows, re_cols] = n_re
            bu_ref[rows, im_cols] = n_im
            return n_re, n_im

        s_re, s_im = lax.fori_loop(0, steps, step, (st_ref[:, re_cols], st_ref[:, im_cols]),
                                   unroll=unroll)
        st_ref[:, re_cols] = s_re
        st_ref[:, im_cols] = s_im

    ys = []
    for k in range(nb):
        s_re = bu_ref[:, k * width:k * width + half].astype(BF16)
        s_im = bu_ref[:, k * width + half:(k + 1) * width].astype(BF16)
        ys.append(jnp.dot(s_re, cre_ref[k], preferred_element_type=F32)
                  - jnp.dot(s_im, cim_ref[k], preferred_element_type=F32))
    y = jnp.concatenate(ys, axis=-1) + d_ref[...] * u
    y = jax.nn.gelu(y)
    z = jnp.dot(y.astype(BF16), wglu_ref[...], preferred_element_type=F32)
    dm = x.shape[-1]
    o_ref[...] = x + z[:, :dm] * jax.nn.sigmoid(z[:, dm:])


def _s5_layer(h, g, bblk, cre, cim, a_re, a_im, d_skip, w_glu, *, bsz, steps):
    rows_total, dm = h.shape
    rows = bsz * steps
    nb = bblk.shape[0]
    nstate = a_re.shape[1]
    kern = functools.partial(_s5_kernel, nb=nb, bsz=bsz, steps=steps, unroll=4)
    return pl.pallas_call(
        kern,
        out_shape=jax.ShapeDtypeStruct((rows_total, dm), F32),
        grid=(rows_total // rows,),
        in_specs=[pl.BlockSpec((rows, dm), lambda i: (i, 0)),
                  _const_spec((1, dm)), _const_spec(bblk.shape), _const_spec(cre.shape),
                  _const_spec(cim.shape), _const_spec(a_re.shape), _const_spec(a_im.shape),
                  _const_spec((1, dm)), _const_spec(w_glu.shape)],
        out_specs=pl.BlockSpec((rows, dm), lambda i: (i, 0)),
        scratch_shapes=[pltpu.VMEM((rows, 2 * nstate), F32), pltpu.VMEM((bsz, 2 * nstate), F32)],
        compiler_params=_params(), name="s5_layer",
    )(h, g, bblk, cre, cim, a_re, a_im, d_skip, w_glu)


def _ffn_kernel(*refs, shift, halo, seq_tiles, fc, pre_proj, post_norm):
    refs = list(refs)
    h_ref = refs.pop(0)
    if pre_proj:
        oin_ref, wo_ref = refs.pop(0), refs.pop(0)
    g_ref, win_ref, cw_ref, cb_ref, wout_ref = (refs.pop(0) for _ in range(5))
    if post_norm:
        gf_ref = refs.pop(0)
    o_ref, ubuf_ref, halo_ref, act_ref = refs
    tm = h_ref.shape[0]
    ff = act_ref.shape[1]
    x = h_ref[...]
    if pre_proj:
        x = x + jnp.dot(oin_ref[...], wo_ref[...], preferred_element_type=F32)
    hb = _rms(x, g_ref[...]).astype(BF16)

    @pl.when(pl.program_id(0) % seq_tiles == 0)
    def _():
        halo_ref[...] = jnp.zeros_like(halo_ref)

    def conv(u, cols, buf_cols):
        ubuf_ref[0:halo, buf_cols] = halo_ref[:, cols]
        ubuf_ref[halo:halo + tm, buf_cols] = u
        halo_ref[:, cols] = u[tm - halo:, :]
        return (cb_ref[:, cols] + cw_ref[2:3, cols] * u
                + cw_ref[1:2, cols] * ubuf_ref[halo - shift:halo - shift + tm, buf_cols]
                + cw_ref[0:1, cols] * ubuf_ref[halo - 2 * shift:halo - 2 * shift + tm, buf_cols])

    for c in range(ff // fc):
        gate_cols = slice(c * fc, (c + 1) * fc)
        up_cols = slice(ff + c * fc, ff + (c + 1) * fc)
        gate = conv(jnp.dot(hb, win_ref[:, gate_cols], preferred_element_type=F32), gate_cols, slice(0, fc))
        up = conv(jnp.dot(hb, win_ref[:, up_cols], preferred_element_type=F32), up_cols, slice(fc, 2 * fc))
        act_ref[:, gate_cols] = (jax.nn.silu(gate) * up).astype(BF16)

    out = x + jnp.dot(act_ref[...], wout_ref[...], preferred_element_type=F32)
    o_ref[...] = _rms(out, gf_ref[...]) if post_norm else out


def _ffn_layer(h, g, w_in, conv_w, conv_b, w_out, *, tm, shift, seq_tiles, fc, attn_out=None, w_o=None,
               g_final=None):
    rows_total, dm = h.shape
    ff = w_out.shape[0]
    halo = max(V7X_SUBLANES, 2 * shift)
    row_spec = pl.BlockSpec((tm, dm), lambda i: (i, 0))
    pre_proj, post_norm = attn_out is not None, g_final is not None
    args, specs = [h], [row_spec]
    if pre_proj:
        args += [attn_out, w_o]
        specs += [row_spec, _const_spec(w_o.shape)]
    args += [g, w_in, conv_w, conv_b, w_out]
    specs += [_const_spec((1, dm)), _const_spec(w_in.shape), _const_spec(conv_w.shape),
              _const_spec(conv_b.shape), _const_spec(w_out.shape)]
    if post_norm:
        args.append(g_final)
        specs.append(_const_spec((1, dm)))
    kern = functools.partial(_ffn_kernel, shift=shift, halo=halo, seq_tiles=seq_tiles, fc=fc,
                             pre_proj=pre_proj, post_norm=post_norm)
    return pl.pallas_call(
        kern,
        out_shape=jax.ShapeDtypeStruct((rows_total, dm), F32),
        grid=(rows_total // tm,),
        in_specs=specs,
        out_specs=row_spec,
        scratch_shapes=[pltpu.VMEM((tm + halo, 2 * fc), F32), pltpu.VMEM((halo, 2 * ff), F32),
                        pltpu.VMEM((tm, ff), BF16)],
        compiler_params=_params(), name="conv_ffn",
    )(*args)


def _norm_proj_kernel(h_ref, g_ref, w_ref, *o_refs, scale):
    hb = _rms(h_ref[...], g_ref[...]).astype(BF16)
    z = jnp.dot(hb, w_ref[...], preferred_element_type=F32)
    col = 0
    for o_ref in o_refs:
        n = o_ref.shape[1]
        o_ref[...] = (z[:, col:col + n] * scale).astype(o_ref.dtype)
        col += n


def _norm_proj(h, g, w, out_widths, out_dtypes, *, tm, scale=1.0):
    rows_total, dm = h.shape
    return pl.pallas_call(
        functools.partial(_norm_proj_kernel, scale=scale),
        out_shape=tuple(jax.ShapeDtypeStruct((rows_total, n), dt) for n, dt in zip(out_widths, out_dtypes)),
        grid=(rows_total // tm,),
        in_specs=[pl.BlockSpec((tm, dm), lambda i: (i, 0)), _const_spec((1, dm)), _const_spec(w.shape)],
        out_specs=tuple(pl.BlockSpec((tm, n), lambda i: (i, 0)) for n in out_widths),
        compiler_params=_params(), name="norm_proj",
    )(h, g, w)


LOG2E = 1.4426950408889634
N_SPLIT = 3


def _cumsum_kernel(f_ref, b_ref, c_ref, *part_refs):
    x = jax.nn.log_sigmoid(f_ref[...] + b_ref[...])
    n = x.shape[1]
    lane = lax.broadcasted_iota(jnp.int32, x.shape, 1)
    shift = 1
    while shift < n:
        x = x + jnp.where(lane >= shift, pltpu.roll(x, shift, axis=1), 0.0)
        shift *= 2
    c = x * LOG2E
    c_ref[...] = c
    rest = -c
    for ref in part_refs:
        piece = rest.astype(BF16)
        ref[...] = piece
        rest = rest - piece.astype(F32)


def _forget_cumsum(f_t, b_col):
    return pl.pallas_call(
        _cumsum_kernel,
        out_shape=(jax.ShapeDtypeStruct(f_t.shape, F32),) + (jax.ShapeDtypeStruct(f_t.shape, BF16),) * N_SPLIT,
        name="forget_cumsum")(f_t, b_col)


ONES_ROWS = 16
QUERY_GROUP = 512


def _attn_kernel(q_ref, k_ref, kaug_ref, vt_ref, cq_ref, o_ref, z0_ref, z1_ref, acc_ref, ws_ref, *, hd):
    nq, tq = cq_ref.shape[2], cq_ref.shape[4]
    tk = vt_ref.shape[4]
    keep = (lax.broadcasted_iota(jnp.int32, (tk, tq), 0)
            <= lax.broadcasted_iota(jnp.int32, (tk, tq), 1))

    def tile_rows(t):
        return pl.ds(pl.multiple_of(t * tq, tq), tq)

    def prep_tile(t):
        qt = q_ref[0, tile_rows(t), :].astype(F32).T
        row = lax.broadcasted_iota(jnp.int32, qt.shape, 0)
        for hh in range(2):
            qm = jnp.where((row >= hh * hd) & (row < (hh + 1) * hd), qt, 0.0)
            aw = jnp.where((row >= N_SPLIT * hh) & (row < N_SPLIT * (hh + 1)), 1.0, 0.0)
            ws_ref[hh] = jnp.concatenate([qm, aw], axis=0).astype(BF16)

    def key_rows(j):
        rows = pl.ds(pl.multiple_of(j * tk, tk), tk)
        return jnp.concatenate([k_ref[0, rows, :], kaug_ref[0, 0, rows, :]], axis=1)

    groups = tuple((hh, slice(g * QUERY_GROUP, (g + 1) * QUERY_GROUP))
                   for hh in range(2) for g in range(tq // QUERY_GROUP))

    def scores_group(z_ref, kx, hh, lanes):
        z = jnp.dot(kx, ws_ref[hh, :, lanes], preferred_element_type=F32)
        z_ref[hh, :, lanes] = z
        return jnp.max(z, axis=0, keepdims=True)

    def consume_group(z_ref, vt, cq, hh, lanes, zm, m, masked):
        z, c_row = z_ref[hh, :, lanes], cq[hh:hh + 1, lanes]
        if masked:
            z = jnp.where(keep[:, lanes], z, -jnp.inf)
            zm = jnp.max(z, axis=0, keepdims=True)
        m_new = jnp.maximum(m, zm + c_row)
        alpha = jnp.exp2(m - m_new)
        p = jnp.exp2(z - (m_new - c_row)).astype(BF16)
        v_ext = jnp.concatenate([vt[hh * hd:(hh + 1) * hd, :], jnp.ones((ONES_ROWS, tk), BF16)], axis=0)
        acc_ref[hh, :, lanes] = alpha * acc_ref[hh, :, lanes] + jnp.dot(v_ext, p, preferred_element_type=F32)
        return m_new

    zbufs = (z0_ref, z1_ref)
    fresh_stats = tuple(jnp.full((1, QUERY_GROUP), NEG_BIG, F32) for _ in groups)

    def inner_body(t, j, zmax, stats, pending):
        kx, vt, cq = key_rows(j + 1), vt_ref[0, 0, j], cq_ref[0, 0, t]
        new_zmax, new_stats = [], []
        for g, (hh, lanes) in enumerate(groups):
            new_zmax.append(scores_group(zbufs[1 - pending], kx, hh, lanes))
            new_stats.append(consume_group(zbufs[pending], vt, cq, hh, lanes, zmax[g], stats[g], False))
        return tuple(new_zmax), tuple(new_stats)

    def last_body(t, stats, pending):
        nxt = jnp.minimum(t + 1, nq - 1)
        prep_tile(nxt)
        kx, vt, cq = key_rows(0), vt_ref[0, 0, t], cq_ref[0, 0, t]
        new_zmax = []
        for g, (hh, lanes) in enumerate(groups):
            new_zmax.append(scores_group(zbufs[1 - pending], kx, hh, lanes))
            consume_group(zbufs[pending], vt, cq, hh, lanes, None, stats[g], True)
        out_t = jnp.concatenate([acc_ref[hh, 0:hd, :] / acc_ref[hh, hd:hd + 1, :] for hh in range(2)],
                                axis=0)
        o_ref[0, tile_rows(t), :] = out_t.T.astype(o_ref.dtype)
        acc_ref[...] = jnp.zeros_like(acc_ref)
        return tuple(new_zmax)

    def by_parity(fn, pending, *args):
        return lax.cond(pending == 0, functools.partial(fn, pending=0), functools.partial(fn, pending=1), *args)

    def tile_body(t, carry):
        zmax, pending = carry

        def step(j, inner):
            zmax, stats, pending = inner
            zmax, stats = by_parity(inner_body, pending, t, j, zmax, stats)
            return zmax, stats, 1 - pending

        zmax, stats, pending = lax.fori_loop(0, t, step, (zmax, fresh_stats, pending))
        return by_parity(last_body, pending, t, stats), 1 - pending

    prep_tile(0)
    acc_ref[...] = jnp.zeros_like(acc_ref)
    kx0 = key_rows(0)
    zmax0 = tuple(scores_group(z0_ref, kx0, hh, lanes) for hh, lanes in groups)
    lax.fori_loop(0, nq, tile_body, (zmax0, jnp.int32(0)))


def _attention(q, k, kaug, vt, cq, *, hd):
    bsz, seq, dm = q.shape
    npair, nk, tq = vt.shape[1], vt.shape[2], vt.shape[4]
    pair_spec = pl.BlockSpec((1, seq, V7X_LANES), lambda b, p: (b, 0, p))
    return pl.pallas_call(
        functools.partial(_attn_kernel, hd=hd),
        out_shape=jax.ShapeDtypeStruct(q.shape, BF16),
        grid=(bsz, npair),
        in_specs=[pair_spec, pair_spec,
                  pl.BlockSpec((1, 1, seq, V7X_LANES), lambda b, p: (b, p, 0, 0)),
                  pl.BlockSpec((1, 1, nk, V7X_LANES, tq), lambda b, p: (b, p, 0, 0, 0)),
                  pl.BlockSpec((1, 1, nk, 2, tq), lambda b, p: (b, p, 0, 0, 0))],
        out_specs=pair_spec,
        scratch_shapes=[pltpu.VMEM((2, tq, tq), F32), pltpu.VMEM((2, tq, tq), F32),
                        pltpu.VMEM((2, hd + ONES_ROWS, tq), F32), pltpu.VMEM((2, 2 * V7X_LANES, tq), BF16)],
        compiler_params=_params(2), name="fox_attention",
    )(q, k, kaug, vt, cq)


def kernel(x, g_mix, g_ffn, lam_re, lam_im, log_dt, ssm_b_re, ssm_b_im, ssm_c_re, ssm_c_im, ssm_d,
           w_glu, g_kv, w_kvf, b_f, w_q, w_o, w_ffn_in, ffn_conv_w, ffn_conv_b, w_ffn_out, g_final):
    bsz, seq, dm = x.shape
    depth = g_mix.shape[0]
    n_a = lam_re.shape[0]
    nh = b_f.shape[0]
    hd = dm // nh
    ff = w_ffn_out.shape[1]
    assert bsz == V7X_SUBLANES and 2 * hd == V7X_LANES

    tm = min(512, seq)
    steps = tm // bsz
    fc = 256 if ff % 256 == 0 else V7X_LANES
    tq = min(512, seq)
    row = lambda a: a.reshape(1, -1)

    w_in_b = w_ffn_in.astype(BF16)
    w_out_b = w_ffn_out.astype(BF16)

    def ffn(h, layer, shift, seq_tiles, **fused):
        return _ffn_layer(h, row(g_ffn[layer]), w_in_b[layer], ffn_conv_w[layer], row(ffn_conv_b[layer]),
                          w_out_b[layer], tm=tm, shift=shift, seq_tiles=seq_tiles, fc=fc, **fused)

    h = x.transpose(1, 0, 2).reshape(seq * bsz, dm)
    for layer in range(n_a):
        lb_re, lb_im, bb_re, bb_im = _s5_prep(lam_re[layer], lam_im[layer], log_dt[layer],
                                              ssm_b_re[layer], ssm_b_im[layer])
        h = _s5_layer(h, row(g_mix[layer]), _block_diag_in(bb_re, bb_im).astype(BF16),
                      _block_diag_out(ssm_c_re[layer]).astype(BF16),
                      _block_diag_out(ssm_c_im[layer]).astype(BF16),
                      row(lb_re), row(lb_im), row(ssm_d[layer]), w_glu[layer].astype(BF16),
                      bsz=bsz, steps=steps)
        h = ffn(h, layer, bsz, seq * bsz // tm)

    h = h.reshape(seq, bsz, dm).transpose(1, 0, 2).reshape(bsz * seq, dm)
    fpad = V7X_LANES
    w_kvf_b = jnp.pad(w_kvf, ((0, 0), (0, fpad - nh))).astype(BF16)
    k, v, f_logit = _norm_proj(h, row(g_kv), w_kvf_b, (dm, dm, fpad), (BF16, BF16, F32), tm=tm)
    f_t = f_logit[:, :nh].reshape(bsz, seq, nh).transpose(0, 2, 1).reshape(bsz * nh, seq)
    cum_t, *parts = _forget_cumsum(f_t, jnp.tile(b_f, bsz).reshape(bsz * nh, 1))
    npair = nh // 2
    nk = seq // tq
    cq = cum_t.reshape(bsz, npair, 2, nk, tq).transpose(0, 1, 3, 2, 4)
    kaug = jnp.stack(parts, axis=1).reshape(bsz, npair, 2 * N_SPLIT, seq).transpose(0, 1, 3, 2)
    kaug = jnp.pad(kaug, ((0, 0), (0, 0), (0, 0), (0, V7X_LANES - 2 * N_SPLIT)))
    k3 = k.reshape(bsz, seq, dm)
    vt = v.reshape(bsz, nk, tq, npair, V7X_LANES).transpose(0, 3, 1, 4, 2)
    for layer in range(n_a, depth):
        j = layer - n_a
        (q,) = _norm_proj(h, row(g_mix[layer]), w_q[j].astype(BF16), (dm,), (BF16,), tm=tm,
                          scale=hd ** -0.5 * LOG2E)
        o = _attention(q.reshape(bsz, seq, dm), k3, kaug, vt, cq, hd=hd)
        h = ffn(h, layer, 1, seq // tm, attn_out=o.reshape(bsz * seq, dm), w_o=w_o[j].astype(BF16),
                g_final=row(g_final) if layer == depth - 1 else None)
    return h.reshape(bsz, seq, dm)
```

```python
import functools

import jax
import jax.numpy as jnp
from jax import lax
from jax.experimental import pallas as pl
from jax.experimental.pallas import tpu as pltpu

EPS = 1e-6
V7X_LANES = 128
V7X_SUBLANES = 8
V7X_VMEM_LIMIT_BYTES = 56 * 1024 * 1024
F32 = jnp.float32
BF16 = jnp.bfloat16
NEG_BIG = -1e30


def _params(n_axes=1):
    return pltpu.CompilerParams(dimension_semantics=("arbitrary",) * n_axes,
                                vmem_limit_bytes=V7X_VMEM_LIMIT_BYTES)


def _const_spec(shape):
    nd = len(shape)
    return pl.BlockSpec(shape, lambda *_: (0,) * nd, pipeline_mode=pl.Buffered(1))


def _rms(x, g):
    return x * lax.rsqrt(jnp.mean(x * x, axis=-1, keepdims=True) + EPS) * g


def _s5_prep_kernel(lr_ref, li_ref, ldt_ref, br_ref, bi_ref, lbr_ref, lbi_ref, bbr_ref, bbi_ref):
    lr = lr_ref[...]
    li = li_ref[...]
    dt = jnp.exp(ldt_ref[...])
    mag = jnp.exp(lr * dt)
    lb_re = mag * jnp.cos(li * dt)
    lb_im = mag * jnp.sin(li * dt)
    den = lr * lr + li * li
    nr = lb_re - 1.0
    fr = (nr * lr + lb_im * li) / den
    fi = (lb_im * lr - nr * li) / den
    br = br_ref[...]
    bi = bi_ref[...]
    lbr_ref[...] = lb_re
    lbi_ref[...] = lb_im
    bbr_ref[...] = fr * br - fi * bi
    bbi_ref[...] = fr * bi + fi * br


def _s5_prep(lam_re, lam_im, log_dt, b_re, b_im):
    g, p, h = b_re.shape
    rep = lambda a: jnp.repeat(a, h, axis=0)
    args = (rep(lam_re), rep(lam_im), rep(jnp.broadcast_to(log_dt[:, None], (g, p))),
            b_re.transpose(0, 2, 1).reshape(g * h, p), b_im.transpose(0, 2, 1).reshape(g * h, p))
    shp = jax.ShapeDtypeStruct((g * h, p), F32)
    lbr, lbi, bbr, bbi = pl.pallas_call(_s5_prep_kernel, out_shape=(shp,) * 4, name="s5_prep")(*args)
    lb_re = lbr.reshape(g, h, p)[:, 0, :]
    lb_im = lbi.reshape(g, h, p)[:, 0, :]
    return lb_re, lb_im, bbr.reshape(g, h, p), bbi.reshape(g, h, p)


def _block_diag_in(bb_re, bb_im):
    g, h, p = bb_re.shape
    gpb = V7X_LANES // h
    nb = g // gpb
    eye = jnp.eye(gpb, dtype=F32)
    bb = jnp.stack([bb_re, bb_im]).reshape(2, nb, gpb, h, p)
    return jnp.einsum("rkghp,gj->kghrjp", bb, eye).reshape(nb, gpb * h, 2 * gpb * p)


def _block_diag_out(c):
    g, h, p = c.shape
    gpb = V7X_LANES // h
    nb = g // gpb
    eye = jnp.eye(gpb, dtype=F32)
    return jnp.einsum("kghp,gj->kjpgh", c.reshape(nb, gpb, h, p), eye).reshape(nb, gpb * p, gpb * h)


def _s5_kernel(h_ref, g_ref, bblk_ref, cre_ref, cim_ref, are_ref, aim_ref, d_ref, wglu_ref,
               o_ref, bu_ref, st_ref, *, nb, bsz, steps, unroll):
    half = are_ref.shape[1] // nb
    width = 2 * half

    @pl.when(pl.program_id(0) == 0)
    def _():
        st_ref[...] = jnp.zeros_like(st_ref)

    x = h_ref[...]
    if x.ndim == 3:
        x = jnp.swapaxes(x, 0, 1).reshape(bsz * steps, x.shape[2])
    u = _rms(x, g_ref[...])
    ub = u.astype(BF16)
    for k in range(nb):
        bu_ref[:, k * width:(k + 1) * width] = jnp.dot(
            ub[:, k * V7X_LANES:(k + 1) * V7X_LANES], bblk_ref[k], preferred_element_type=F32)

    for k in range(nb):
        re_cols = slice(k * width, k * width + half)
        im_cols = slice(k * width + half, (k + 1) * width)
        a_re = jnp.broadcast_to(are_ref[:, k * half:(k + 1) * half], (bsz, half))
        a_im = jnp.broadcast_to(aim_ref[:, k * half:(k + 1) * half], (bsz, half))

        def step(t, carry, re_cols=re_cols, im_cols=im_cols, a_re=a_re, a_im=a_im):
            s_re, s_im = carry
            rows = pl.ds(pl.multiple_of(t * bsz, bsz), bsz)
            n_re = a_re * s_re - a_im * s_im + bu_ref[rows, re_cols]
            n_im = a_re * s_im + a_im * s_re + bu_ref[rows, im_cols]
            bu_ref[rows, re_cols] = n_re
            bu_ref[rows, im_cols] = n_im
            return n_re, n_im

        s_re, s_im = lax.fori_loop(0, steps, step, (st_ref[:, re_cols], st_ref[:, im_cols]),
                                   unroll=unroll)
        st_ref[:, re_cols] = s_re
        st_ref[:, im_cols] = s_im

    ys = []
    for k in range(nb):
        s_re = bu_ref[:, k * width:k * width + half].astype(BF16)
        s_im = bu_ref[:, k * width + half:(k + 1) * width].astype(BF16)
        ys.append(jnp.dot(s_re, cre_ref[k], preferred_element_type=F32)
                  - jnp.dot(s_im, cim_ref[k], preferred_element_type=F32))
    y = jnp.concatenate(ys, axis=-1) + d_ref[...] * u
    y = jax.nn.gelu(y)
    z = jnp.dot(y.astype(BF16), wglu_ref[...], preferred_element_type=F32)
    dm = x.shape[-1]
    o_ref[...] = x + z[:, :dm] * jax.nn.sigmoid(z[:, dm:])


def _s5_layer(h, g, bblk, cre, cim, a_re, a_im, d_skip, w_glu, *, bsz, steps):
    dm = h.shape[-1]
    rows_total = h.size // dm
    rows = bsz * steps
    nb = bblk.shape[0]
    nstate = a_re.shape[1]
    kern = functools.partial(_s5_kernel, nb=nb, bsz=bsz, steps=steps, unroll=4)
    h_spec = (pl.BlockSpec((rows, dm), lambda i: (i, 0)) if h.ndim == 2
              else pl.BlockSpec((bsz, steps, dm), lambda i: (0, i, 0)))
    return pl.pallas_call(
        kern,
        out_shape=jax.ShapeDtypeStruct((rows_total, dm), F32),
        grid=(rows_total // rows,),
        in_specs=[h_spec,
                  _const_spec((1, dm)), _const_spec(bblk.shape), _const_spec(cre.shape),
                  _const_spec(cim.shape), _const_spec(a_re.shape), _const_spec(a_im.shape),
                  _const_spec((1, dm)), _const_spec(w_glu.shape)],
        out_specs=pl.BlockSpec((rows, dm), lambda i: (i, 0)),
        scratch_shapes=[pltpu.VMEM((rows, 2 * nstate), F32), pltpu.VMEM((bsz, 2 * nstate), F32)],
        compiler_params=_params(), name="s5_layer",
    )(h, g, bblk, cre, cim, a_re, a_im, d_skip, w_glu)


def _ffn_kernel(*refs, shift, halo, seq_tiles, fc, pre_proj, post_norm):
    refs = list(refs)
    h_ref = refs.pop(0)
    if pre_proj:
        oin_ref, wo_ref = refs.pop(0), refs.pop(0)
    g_ref, win_ref, cw_ref, cb_ref, wout_ref = (refs.pop(0) for _ in range(5))
    if post_norm:
        gf_ref = refs.pop(0)
    o_ref, ubuf_ref, halo_ref, act_ref = refs
    tm = h_ref.shape[0]
    ff = act_ref.shape[1]
    x = h_ref[...]
    if pre_proj:
        x = x + jnp.dot(oin_ref[...], wo_ref[...], preferred_element_type=F32)
    hb = _rms(x, g_ref[...]).astype(BF16)

    @pl.when(pl.program_id(0) % seq_tiles == 0)
    def _():
        halo_ref[...] = jnp.zeros_like(halo_ref)

    def conv(u, cols, buf_cols):
        ubuf_ref[0:halo, buf_cols] = halo_ref[:, cols]
        ubuf_ref[halo:halo + tm, buf_cols] = u
        halo_ref[:, cols] = u[tm - halo:, :]
        return (cb_ref[:, cols] + cw_ref[2:3, cols] * u
                + cw_ref[1:2, cols] * ubuf_ref[halo - shift:halo - shift + tm, buf_cols]
                + cw_ref[0:1, cols] * ubuf_ref[halo - 2 * shift:halo - 2 * shift + tm, buf_cols])

    for c in range(ff // fc):
        gate_cols = slice(c * fc, (c + 1) * fc)
        up_cols = slice(ff + c * fc, ff + (c + 1) * fc)
        gate = conv(jnp.dot(hb, win_ref[:, gate_cols], preferred_element_type=F32), gate_cols, slice(0, fc))
        up = conv(jnp.dot(hb, win_ref[:, up_cols], preferred_element_type=F32), up_cols, slice(fc, 2 * fc))
        act_ref[:, gate_cols] = (jax.nn.silu(gate) * up).astype(BF16)

    out = x + jnp.dot(act_ref[...], wout_ref[...], preferred_element_type=F32)
    if post_norm:
        out = _rms(out, gf_ref[...])
    if len(o_ref.shape) == 3:
        nb = o_ref.shape[0]
        out = jnp.swapaxes(out.reshape(tm // nb, nb, out.shape[1]), 0, 1)
    o_ref[...] = out


def _ffn_layer(h, g, w_in, conv_w, conv_b, w_out, *, tm, shift, seq_tiles, fc, attn_out=None, w_o=None,
               g_final=None, batch_major_out=None):
    rows_total, dm = h.shape
    ff = w_out.shape[0]
    halo = max(V7X_SUBLANES, 2 * shift)
    row_spec = pl.BlockSpec((tm, dm), lambda i: (i, 0))
    if batch_major_out is None:
        out_shape, out_spec = (rows_total, dm), row_spec
    else:
        out_shape = (batch_major_out, rows_total // batch_major_out, dm)
        out_spec = pl.BlockSpec((batch_major_out, tm // batch_major_out, dm), lambda i: (0, i, 0))
    pre_proj, post_norm = attn_out is not None, g_final is not None
    args, specs = [h], [row_spec]
    if pre_proj:
        args += [attn_out, w_o]
        specs += [row_spec, _const_spec(w_o.shape)]
    args += [g, w_in, conv_w, conv_b, w_out]
    specs += [_const_spec((1, dm)), _const_spec(w_in.shape), _const_spec(conv_w.shape),
              _const_spec(conv_b.shape), _const_spec(w_out.shape)]
    if post_norm:
        args.append(g_final)
        specs.append(_const_spec((1, dm)))
    kern = functools.partial(_ffn_kernel, shift=shift, halo=halo, seq_tiles=seq_tiles, fc=fc,
                             pre_proj=pre_proj, post_norm=post_norm)
    return pl.pallas_call(
        kern,
        out_shape=jax.ShapeDtypeStruct(out_shape, F32),
        grid=(rows_total // tm,),
        in_specs=specs,
        out_specs=out_spec,
        scratch_shapes=[pltpu.VMEM((tm + halo, 2 * fc), F32), pltpu.VMEM((halo, 2 * ff), F32),
                        pltpu.VMEM((tm, ff), BF16)],
        compiler_params=_params(), name="conv_ffn",
    )(*args)


def _norm_proj_kernel(h_ref, g_ref, w_ref, *o_refs, scale):
    hb = _rms(h_ref[...], g_ref[...]).astype(BF16)
    z = jnp.dot(hb, w_ref[...], preferred_element_type=F32)
    col = 0
    for o_ref in o_refs:
        n = o_ref.shape[1]
        o_ref[...] = (z[:, col:col + n] * scale).astype(o_ref.dtype)
        col += n


def _norm_proj(h, g, w, out_widths, out_dtypes, *, tm, scale=1.0):
    rows_total, dm = h.shape
    return pl.pallas_call(
        functools.partial(_norm_proj_kernel, scale=scale),
        out_shape=tuple(jax.ShapeDtypeStruct((rows_total, n), dt) for n, dt in zip(out_widths, out_dtypes)),
        grid=(rows_total // tm,),
        in_specs=[pl.BlockSpec((tm, dm), lambda i: (i, 0)), _const_spec((1, dm)), _const_spec(w.shape)],
        out_specs=tuple(pl.BlockSpec((tm, n), lambda i: (i, 0)) for n in out_widths),
        compiler_params=_params(), name="norm_proj",
    )(h, g, w)


def _kv_proj_kernel(h_ref, g_ref, w_ref, k_ref, vt_ref, f_ref):
    dm = k_ref.shape[1]
    hb = _rms(h_ref[...], g_ref[...]).astype(BF16)
    z = jnp.dot(hb, w_ref[...], preferred_element_type=F32)
    k_ref[...] = z[:, :dm].astype(k_ref.dtype)
    for p in range(vt_ref.shape[1]):
        lanes = slice(dm + p * V7X_LANES, dm + (p + 1) * V7X_LANES)
        vt_ref[0, p, 0] = z[:, lanes].T.astype(vt_ref.dtype)
    f_ref[...] = z[:, 2 * dm:]


def _kv_proj(h, g, w, *, bsz, tm):
    rows_total, dm = h.shape
    npair = dm // V7X_LANES
    nk = rows_total // bsz // tm
    fpad = w.shape[1] - 2 * dm
    row_spec = lambda n: pl.BlockSpec((tm, n), lambda i: (i, 0))
    return pl.pallas_call(
        _kv_proj_kernel,
        out_shape=(jax.ShapeDtypeStruct((rows_total, dm), BF16),
                   jax.ShapeDtypeStruct((bsz, npair, nk, V7X_LANES, tm), BF16),
                   jax.ShapeDtypeStruct((rows_total, fpad), F32)),
        grid=(rows_total // tm,),
        in_specs=[row_spec(dm), _const_spec((1, dm)), _const_spec(w.shape)],
        out_specs=(row_spec(dm),
                   pl.BlockSpec((1, npair, 1, V7X_LANES, tm), lambda i: (i // nk, 0, i % nk, 0, 0)),
                   row_spec(fpad)),
        compiler_params=_params(), name="kv_proj",
    )(h, g, w)


LOG2E = 1.4426950408889634
N_SPLIT = 3


def _cumsum_kernel(f_ref, b_ref, c_ref, *part_refs):
    x = jax.nn.log_sigmoid(f_ref[...] + b_ref[...])
    n = x.shape[1]
    lane = lax.broadcasted_iota(jnp.int32, x.shape, 1)
    shift = 1
    while shift < n:
        x = x + jnp.where(lane >= shift, pltpu.roll(x, shift, axis=1), 0.0)
        shift *= 2
    c = x * LOG2E
    c_ref[...] = c
    rest = -c
    for ref in part_refs:
        piece = rest.astype(BF16)
        ref[...] = piece
        rest = rest - piece.astype(F32)


def _forget_cumsum(f_t, b_col):
    return pl.pallas_call(
        _cumsum_kernel,
        out_shape=(jax.ShapeDtypeStruct(f_t.shape, F32),) + (jax.ShapeDtypeStruct(f_t.shape, BF16),) * N_SPLIT,
        name="forget_cumsum")(f_t, b_col)


ONES_ROWS = 16
QUERY_GROUP = 512
ORDER = (("s", 0), ("c", 0), ("s", 1), ("c", 1))


def _attn_kernel(q_ref, k_ref, kaug_ref, vt_ref, cq_ref, o_ref, z0_ref, z1_ref, acc_ref, ws_ref, *, hd):
    nq, tq = cq_ref.shape[2], cq_ref.shape[4]
    tk = vt_ref.shape[4]
    keep = (lax.broadcasted_iota(jnp.int32, (tk, tq), 0)
            <= lax.broadcasted_iota(jnp.int32, (tk, tq), 1))

    def tile_rows(t):
        return pl.ds(pl.multiple_of(t * tq, tq), tq)

    def prep_tile(t):
        qt = q_ref[0, tile_rows(t), :].astype(F32).T
        row = lax.broadcasted_iota(jnp.int32, qt.shape, 0)
        for hh in range(2):
            qm = jnp.where((row >= hh * hd) & (row < (hh + 1) * hd), qt, 0.0)
            aw = jnp.where((row >= N_SPLIT * hh) & (row < N_SPLIT * (hh + 1)), 1.0, 0.0)
            ws_ref[hh] = jnp.concatenate([qm, aw], axis=0).astype(BF16)

    def key_rows(j):
        rows = pl.ds(pl.multiple_of(j * tk, tk), tk)
        return jnp.concatenate([k_ref[0, rows, :], kaug_ref[0, 0, rows, :]], axis=1)

    groups = tuple((hh, slice(g * QUERY_GROUP, (g + 1) * QUERY_GROUP))
                   for hh in range(2) for g in range(tq // QUERY_GROUP))

    def scores_group(z_ref, kx, hh, lanes):
        z = jnp.dot(kx, ws_ref[hh, :, lanes], preferred_element_type=F32)
        z_ref[hh, :, lanes] = z
        return jnp.max(z, axis=0, keepdims=True)

    def consume_group(z_ref, vt, cq, hh, lanes, zm, m, masked):
        z, c_row = z_ref[hh, :, lanes], cq[hh:hh + 1, lanes]
        if masked:
            z = jnp.where(keep[:, lanes], z, -jnp.inf)
            zm = jnp.max(z, axis=0, keepdims=True)
        m_new = jnp.maximum(m, zm + c_row)
        alpha = jnp.exp2(m - m_new)
        p = jnp.exp2(z - (m_new - c_row)).astype(BF16)
        v_ext = jnp.concatenate([vt[hh * hd:(hh + 1) * hd, :], jnp.ones((ONES_ROWS, tk), BF16)], axis=0)
        acc_ref[hh, :, lanes] = alpha * acc_ref[hh, :, lanes] + jnp.dot(v_ext, p, preferred_element_type=F32)
        return m_new

    zbufs = (z0_ref, z1_ref)
    fresh_stats = tuple(jnp.full((1, QUERY_GROUP), NEG_BIG, F32) for _ in groups)

    def inner_body(t, j, zmax, stats, pending):
        kx, vt, cq = key_rows(j + 1), vt_ref[0, 0, j], cq_ref[0, 0, t]
        new_zmax, new_stats = [None, None], [None, None]
        for op, g in ORDER:
            hh, lanes = groups[g]
            if op == "s":
                new_zmax[g] = scores_group(zbufs[1 - pending], kx, hh, lanes)
            else:
                new_stats[g] = consume_group(zbufs[pending], vt, cq, hh, lanes, zmax[g], stats[g], False)
        return tuple(new_zmax), tuple(new_stats)

    def last_body(t, stats, pending):
        nxt = jnp.minimum(t + 1, nq - 1)
        prep_tile(nxt)
        kx, vt, cq = key_rows(0), vt_ref[0, 0, t], cq_ref[0, 0, t]
        new_zmax = []
        for g, (hh, lanes) in enumerate(groups):
            new_zmax.append(scores_group(zbufs[1 - pending], kx, hh, lanes))
            consume_group(zbufs[pending], vt, cq, hh, lanes, None, stats[g], True)
        out_t = jnp.concatenate([acc_ref[hh, 0:hd, :] / acc_ref[hh, hd:hd + 1, :] for hh in range(2)],
                                axis=0)
        o_ref[0, tile_rows(t), :] = out_t.T.astype(o_ref.dtype)
        acc_ref[...] = jnp.zeros_like(acc_ref)
        return tuple(new_zmax)

    def by_parity(fn, pending, *args):
        return lax.cond(pending == 0, functools.partial(fn, pending=0), functools.partial(fn, pending=1), *args)

    def tile_body(t, carry):
        zmax, pending = carry

        def step(j, inner):
            zmax, stats, pending = inner
            zmax, stats = by_parity(inner_body, pending, t, j, zmax, stats)
            return zmax, stats, 1 - pending

        zmax, stats, pending = lax.fori_loop(0, t, step, (zmax, fresh_stats, pending))
        return by_parity(last_body, pending, t, stats), 1 - pending

    prep_tile(0)
    acc_ref[...] = jnp.zeros_like(acc_ref)
    kx0 = key_rows(0)
    zmax0 = tuple(scores_group(z0_ref, kx0, hh, lanes) for hh, lanes in groups)
    lax.fori_loop(0, nq, tile_body, (zmax0, jnp.int32(0)))


def _attention(q, k, kaug, vt, cq, *, hd):
    bsz, seq, dm = q.shape
    npair, nk, tq = vt.shape[1], vt.shape[2], vt.shape[4]
    pair_spec = pl.BlockSpec((1, seq, V7X_LANES), lambda b, p: (b, 0, p))
    return pl.pallas_call(
        functools.partial(_attn_kernel, hd=hd),
        out_shape=jax.ShapeDtypeStruct(q.shape, BF16),
        grid=(bsz, npair),
        in_specs=[pair_spec, pair_spec,
                  pl.BlockSpec((1, 1, seq, V7X_LANES), lambda b, p: (b, p, 0, 0)),
                  pl.BlockSpec((1, 1, nk, V7X_LANES, tq), lambda b, p: (b, p, 0, 0, 0)),
                  pl.BlockSpec((1, 1, nk, 2, tq), lambda b, p: (b, p, 0, 0, 0))],
        out_specs=pair_spec,
        scratch_shapes=[pltpu.VMEM((2, tq, tq), F32), pltpu.VMEM((2, tq, tq), F32),
                        pltpu.VMEM((2, hd + ONES_ROWS, tq), F32), pltpu.VMEM((2, 2 * V7X_LANES, tq), BF16)],
        compiler_params=_params(2), name="fox_attention",
    )(q, k, kaug, vt, cq)


def kernel(x, g_mix, g_ffn, lam_re, lam_im, log_dt, ssm_b_re, ssm_b_im, ssm_c_re, ssm_c_im, ssm_d,
           w_glu, g_kv, w_kvf, b_f, w_q, w_o, w_ffn_in, ffn_conv_w, ffn_conv_b, w_ffn_out, g_final):
    bsz, seq, dm = x.shape
    depth = g_mix.shape[0]
    n_a = lam_re.shape[0]
    nh = b_f.shape[0]
    hd = dm // nh
    ff = w_ffn_out.shape[1]
    assert bsz == V7X_SUBLANES and 2 * hd == V7X_LANES

    tm = min(512, seq)
    steps = tm // bsz
    fc = 256 if ff % 256 == 0 else V7X_LANES
    tq = min(512, seq)
    row = lambda a: a.reshape(1, -1)

    w_in_b = w_ffn_in.astype(BF16)
    w_out_b = w_ffn_out.astype(BF16)

    def ffn(h, layer, shift, seq_tiles, **fused):
        return _ffn_layer(h, row(g_ffn[layer]), w_in_b[layer], ffn_conv_w[layer], row(ffn_conv_b[layer]),
                          w_out_b[layer], tm=tm, shift=shift, seq_tiles=seq_tiles, fc=fc, **fused)

    h = x
    for layer in range(n_a):
        lb_re, lb_im, bb_re, bb_im = _s5_prep(lam_re[layer], lam_im[layer], log_dt[layer],
                                              ssm_b_re[layer], ssm_b_im[layer])
        h = _s5_layer(h, row(g_mix[layer]), _block_diag_in(bb_re, bb_im).astype(BF16),
                      _block_diag_out(ssm_c_re[layer]).astype(BF16),
                      _block_diag_out(ssm_c_im[layer]).astype(BF16),
                      row(lb_re), row(lb_im), row(ssm_d[layer]), w_glu[layer].astype(BF16),
                      bsz=bsz, steps=steps)
        h = ffn(h, layer, bsz, seq * bsz // tm, batch_major_out=bsz if layer == n_a - 1 else None)

    h = h.reshape(bsz * seq, dm)
    fpad = V7X_LANES
    w_kvf_b = jnp.pad(w_kvf, ((0, 0), (0, fpad - nh))).astype(BF16)
    k, vt, f_logit = _kv_proj(h, row(g_kv), w_kvf_b, bsz=bsz, tm=tq)
    f_t = f_logit[:, :nh].reshape(bsz, seq, nh).transpose(0, 2, 1).reshape(bsz * nh, seq)
    cum_t, *parts = _forget_cumsum(f_t, jnp.tile(b_f, bsz).reshape(bsz * nh, 1))
    npair = nh // 2
    nk = seq // tq
    cq = cum_t.reshape(bsz, npair, 2, nk, tq).transpose(0, 1, 3, 2, 4)
    kaug = jnp.stack(parts, axis=1).reshape(bsz, npair, 2 * N_SPLIT, seq).transpose(0, 1, 3, 2)
    kaug = jnp.pad(kaug, ((0, 0), (0, 0), (0, 0), (0, V7X_LANES - 2 * N_SPLIT)))
    k3 = k.reshape(bsz, seq, dm)
    for layer in range(n_a, depth):
        j = layer - n_a
        (q,) = _norm_proj(h, row(g_mix[layer]), w_q[j].astype(BF16), (dm,), (BF16,), tm=tm,
                          scale=hd ** -0.5 * LOG2E)
        o = _attention(q.reshape(bsz, seq, dm), k3, kaug, vt, cq, hd=hd)
        h = ffn(h, layer, 1, seq // tm, attn_out=o.reshape(bsz * seq, dm), w_o=w_o[j].astype(BF16),
                g_final=row(g_final) if layer == depth - 1 else None)
    return h.reshape(bsz, seq, dm)
```

```python
import functools

import jax
import jax.numpy as jnp
from jax import lax
from jax.experimental import pallas as pl
from jax.experimental.pallas import tpu as pltpu

EPS = 1e-6
V7X_LANES = 128
V7X_SUBLANES = 8
V7X_VMEM_LIMIT_BYTES = 56 * 1024 * 1024
F32 = jnp.float32
BF16 = jnp.bfloat16
NEG_BIG = -1e30


def _params(n_axes=1):
    return pltpu.CompilerParams(dimension_semantics=("arbitrary",) * n_axes,
                                vmem_limit_bytes=V7X_VMEM_LIMIT_BYTES)


def _const_spec(shape):
    nd = len(shape)
    return pl.BlockSpec(shape, lambda *_: (0,) * nd, pipeline_mode=pl.Buffered(1))


def _rms(x, g):
    return x * lax.rsqrt(jnp.mean(x * x, axis=-1, keepdims=True) + EPS) * g


def _s5_prep_kernel(lr_ref, li_ref, ldt_ref, br_ref, bi_ref, lbr_ref, lbi_ref, bbr_ref, bbi_ref):
    lr = lr_ref[...]
    li = li_ref[...]
    dt = jnp.exp(ldt_ref[...])
    mag = jnp.exp(lr * dt)
    lb_re = mag * jnp.cos(li * dt)
    lb_im = mag * jnp.sin(li * dt)
    den = lr * lr + li * li
    nr = lb_re - 1.0
    fr = (nr * lr + lb_im * li) / den
    fi = (lb_im * lr - nr * li) / den
    br = br_ref[...]
    bi = bi_ref[...]
    lbr_ref[...] = lb_re
    lbi_ref[...] = lb_im
    bbr_ref[...] = fr * br - fi * bi
    bbi_ref[...] = fr * bi + fi * br


def _s5_prep(lam_re, lam_im, log_dt, b_re, b_im):
    g, p, h = b_re.shape
    rep = lambda a: jnp.repeat(a, h, axis=0)
    args = (rep(lam_re), rep(lam_im), rep(jnp.broadcast_to(log_dt[:, None], (g, p))),
            b_re.transpose(0, 2, 1).reshape(g * h, p), b_im.transpose(0, 2, 1).reshape(g * h, p))
    shp = jax.ShapeDtypeStruct((g * h, p), F32)
    lbr, lbi, bbr, bbi = pl.pallas_call(_s5_prep_kernel, out_shape=(shp,) * 4, name="s5_prep")(*args)
    lb_re = lbr.reshape(g, h, p)[:, 0, :]
    lb_im = lbi.reshape(g, h, p)[:, 0, :]
    return lb_re, lb_im, bbr.reshape(g, h, p), bbi.reshape(g, h, p)


def _block_diag_in(bb_re, bb_im):
    g, h, p = bb_re.shape
    gpb = V7X_LANES // h
    nb = g // gpb
    eye = jnp.eye(gpb, dtype=F32)
    bb = jnp.stack([bb_re, bb_im]).reshape(2, nb, gpb, h, p)
    return jnp.einsum("rkghp,gj->kghrjp", bb, eye).reshape(nb, gpb * h, 2 * gpb * p)


def _block_diag_out(c):
    g, h, p = c.shape
    gpb = V7X_LANES // h
    nb = g // gpb
    eye = jnp.eye(gpb, dtype=F32)
    return jnp.einsum("kghp,gj->kjpgh", c.reshape(nb, gpb, h, p), eye).reshape(nb, gpb * p, gpb * h)


def _s5_kernel(h_ref, g_ref, bblk_ref, cre_ref, cim_ref, are_ref, aim_ref, d_ref, wglu_ref,
               o_ref, bu_ref, st_ref, *, nb, bsz, steps, unroll):
    half = are_ref.shape[1] // nb
    width = 2 * half

    @pl.when(pl.program_id(0) == 0)
    def _():
        st_ref[...] = jnp.zeros_like(st_ref)

    x = h_ref[...]
    if x.ndim == 3:
        x = jnp.swapaxes(x, 0, 1).reshape(bsz * steps, x.shape[2])
    u = _rms(x, g_ref[...])
    ub = u.astype(BF16)
    for k in range(nb):
        bu_ref[:, k * width:(k + 1) * width] = jnp.dot(
            ub[:, k * V7X_LANES:(k + 1) * V7X_LANES], bblk_ref[k], preferred_element_type=F32)

    for k in range(nb):
        re_cols = slice(k * width, k * width + half)
        im_cols = slice(k * width + half, (k + 1) * width)
        a_re = jnp.broadcast_to(are_ref[:, k * half:(k + 1) * half], (bsz, half))
        a_im = jnp.broadcast_to(aim_ref[:, k * half:(k + 1) * half], (bsz, half))

        def step(t, carry, re_cols=re_cols, im_cols=im_cols, a_re=a_re, a_im=a_im):
            s_re, s_im = carry
            rows = pl.ds(pl.multiple_of(t * bsz, bsz), bsz)
            n_re = a_re * s_re - a_im * s_im + bu_ref[rows, re_cols]
            n_im = a_re * s_im + a_im * s_re + bu_ref[rows, im_cols]
            bu_ref[rows, re_cols] = n_re
            bu_ref[rows, im_cols] = n_im
            return n_re, n_im

        s_re, s_im = lax.fori_loop(0, steps, step, (st_ref[:, re_cols], st_ref[:, im_cols]),
                                   unroll=unroll)
        st_ref[:, re_cols] = s_re
        st_ref[:, im_cols] = s_im

    ys = []
    for k in range(nb):
        s_re = bu_ref[:, k * width:k * width + half].astype(BF16)
        s_im = bu_ref[:, k * width + half:(k + 1) * width].astype(BF16)
        ys.append(jnp.dot(s_re, cre_ref[k], preferred_element_type=F32)
                  - jnp.dot(s_im, cim_ref[k], preferred_element_type=F32))
    y = jnp.concatenate(ys, axis=-1) + d_ref[...] * u
    y = jax.nn.gelu(y)
    z = jnp.dot(y.astype(BF16), wglu_ref[...], preferred_element_type=F32)
    dm = x.shape[-1]
    o_ref[...] = x + z[:, :dm] * jax.nn.sigmoid(z[:, dm:])


def _s5_layer(h, g, bblk, cre, cim, a_re, a_im, d_skip, w_glu, *, bsz, steps):
    dm = h.shape[-1]
    rows_total = h.size // dm
    rows = bsz * steps
    nb = bblk.shape[0]
    nstate = a_re.shape[1]
    kern = functools.partial(_s5_kernel, nb=nb, bsz=bsz, steps=steps, unroll=4)
    h_spec = (pl.BlockSpec((rows, dm), lambda i: (i, 0)) if h.ndim == 2
              else pl.BlockSpec((bsz, steps, dm), lambda i: (0, i, 0)))
    return pl.pallas_call(
        kern,
        out_shape=jax.ShapeDtypeStruct((rows_total, dm), F32),
        grid=(rows_total // rows,),
        in_specs=[h_spec,
                  _const_spec((1, dm)), _const_spec(bblk.shape), _const_spec(cre.shape),
                  _const_spec(cim.shape), _const_spec(a_re.shape), _const_spec(a_im.shape),
                  _const_spec((1, dm)), _const_spec(w_glu.shape)],
        out_specs=pl.BlockSpec((rows, dm), lambda i: (i, 0)),
        scratch_shapes=[pltpu.VMEM((rows, 2 * nstate), F32), pltpu.VMEM((bsz, 2 * nstate), F32)],
        compiler_params=_params(), name="s5_layer",
    )(h, g, bblk, cre, cim, a_re, a_im, d_skip, w_glu)


def _ffn_kernel(*refs, shift, halo, seq_tiles, fc, pre_proj, post_norm):
    refs = list(refs)
    h_ref = refs.pop(0)
    if pre_proj:
        oin_ref, wo_ref = refs.pop(0), refs.pop(0)
    g_ref, win_ref, cw_ref, cb_ref, wout_ref = (refs.pop(0) for _ in range(5))
    if post_norm:
        gf_ref = refs.pop(0)
    o_ref, ubuf_ref, halo_ref, act_ref = refs
    tm = h_ref.shape[0]
    ff = act_ref.shape[1]
    x = h_ref[...]
    if pre_proj:
        x = x + jnp.dot(oin_ref[...], wo_ref[...], preferred_element_type=F32)
    hb = _rms(x, g_ref[...]).astype(BF16)

    @pl.when(pl.program_id(0) % seq_tiles == 0)
    def _():
        halo_ref[...] = jnp.zeros_like(halo_ref)

    def conv(u, cols, buf_cols):
        ubuf_ref[0:halo, buf_cols] = halo_ref[:, cols]
        ubuf_ref[halo:halo + tm, buf_cols] = u
        halo_ref[:, cols] = u[tm - halo:, :]
        return (cb_ref[:, cols] + cw_ref[2:3, cols] * u
                + cw_ref[1:2, cols] * ubuf_ref[halo - shift:halo - shift + tm, buf_cols]
                + cw_ref[0:1, cols] * ubuf_ref[halo - 2 * shift:halo - 2 * shift + tm, buf_cols])

    for c in range(ff // fc):
        gate_cols = slice(c * fc, (c + 1) * fc)
        up_cols = slice(ff + c * fc, ff + (c + 1) * fc)
        gate = conv(jnp.dot(hb, win_ref[:, gate_cols], preferred_element_type=F32), gate_cols, slice(0, fc))
        up = conv(jnp.dot(hb, win_ref[:, up_cols], preferred_element_type=F32), up_cols, slice(fc, 2 * fc))
        act_ref[:, gate_cols] = (jax.nn.silu(gate) * up).astype(BF16)

    out = x + jnp.dot(act_ref[...], wout_ref[...], preferred_element_type=F32)
    if post_norm:
        out = _rms(out, gf_ref[...])
    if len(o_ref.shape) == 3:
        nb = o_ref.shape[0]
        out = jnp.swapaxes(out.reshape(tm // nb, nb, out.shape[1]), 0, 1)
    o_ref[...] = out


def _ffn_layer(h, g, w_in, conv_w, conv_b, w_out, *, tm, shift, seq_tiles, fc, attn_out=None, w_o=None,
               g_final=None, batch_major_out=None):
    rows_total, dm = h.shape
    ff = w_out.shape[0]
    halo = max(V7X_SUBLANES, 2 * shift)
    row_spec = pl.BlockSpec((tm, dm), lambda i: (i, 0))
    if batch_major_out is None:
        out_shape, out_spec = (rows_total, dm), row_spec
    else:
        out_shape = (batch_major_out, rows_total // batch_major_out, dm)
        out_spec = pl.BlockSpec((batch_major_out, tm // batch_major_out, dm), lambda i: (0, i, 0))
    pre_proj, post_norm = attn_out is not None, g_final is not None
    args, specs = [h], [row_spec]
    if pre_proj:
        args += [attn_out, w_o]
        specs += [row_spec, _const_spec(w_o.shape)]
    args += [g, w_in, conv_w, conv_b, w_out]
    specs += [_const_spec((1, dm)), _const_spec(w_in.shape), _const_spec(conv_w.shape),
              _const_spec(conv_b.shape), _const_spec(w_out.shape)]
    if post_norm:
        args.append(g_final)
        specs.append(_const_spec((1, dm)))
    kern = functools.partial(_ffn_kernel, shift=shift, halo=halo, seq_tiles=seq_tiles, fc=fc,
                             pre_proj=pre_proj, post_norm=post_norm)
    return pl.pallas_call(
        kern,
        out_shape=jax.ShapeDtypeStruct(out_shape, F32),
        grid=(rows_total // tm,),
        in_specs=specs,
        out_specs=out_spec,
        scratch_shapes=[pltpu.VMEM((tm + halo, 2 * fc), F32), pltpu.VMEM((halo, 2 * ff), F32),
                        pltpu.VMEM((tm, ff), BF16)],
        compiler_params=_params(), name="conv_ffn",
    )(*args)


def _norm_proj_kernel(h_ref, g_ref, w_ref, *o_refs, scale):
    hb = _rms(h_ref[...], g_ref[...]).astype(BF16)
    z = jnp.dot(hb, w_ref[...], preferred_element_type=F32)
    col = 0
    for o_ref in o_refs:
        n = o_ref.shape[1]
        o_ref[...] = (z[:, col:col + n] * scale).astype(o_ref.dtype)
        col += n


def _norm_proj(h, g, w, out_widths, out_dtypes, *, tm, scale=1.0):
    rows_total, dm = h.shape
    return pl.pallas_call(
        functools.partial(_norm_proj_kernel, scale=scale),
        out_shape=tuple(jax.ShapeDtypeStruct((rows_total, n), dt) for n, dt in zip(out_widths, out_dtypes)),
        grid=(rows_total // tm,),
        in_specs=[pl.BlockSpec((tm, dm), lambda i: (i, 0)), _const_spec((1, dm)), _const_spec(w.shape)],
        out_specs=tuple(pl.BlockSpec((tm, n), lambda i: (i, 0)) for n in out_widths),
        compiler_params=_params(), name="norm_proj",
    )(h, g, w)


def _kv_proj_kernel(h_ref, g_ref, w_ref, k_ref, vt_ref, f_ref):
    dm = k_ref.shape[1]
    hb = _rms(h_ref[...], g_ref[...]).astype(BF16)
    z = jnp.dot(hb, w_ref[...], preferred_element_type=F32)
    k_ref[...] = z[:, :dm].astype(k_ref.dtype)
    for p in range(vt_ref.shape[1]):
        lanes = slice(dm + p * V7X_LANES, dm + (p + 1) * V7X_LANES)
        vt_ref[0, p, 0] = z[:, lanes].T.astype(vt_ref.dtype)
    f_ref[...] = z[:, 2 * dm:]


def _kv_proj(h, g, w, *, bsz, tm):
    rows_total, dm = h.shape
    npair = dm // V7X_LANES
    nk = rows_total // bsz // tm
    fpad = w.shape[1] - 2 * dm
    row_spec = lambda n: pl.BlockSpec((tm, n), lambda i: (i, 0))
    return pl.pallas_call(
        _kv_proj_kernel,
        out_shape=(jax.ShapeDtypeStruct((rows_total, dm), BF16),
                   jax.ShapeDtypeStruct((bsz, npair, nk, V7X_LANES, tm), BF16),
                   jax.ShapeDtypeStruct((rows_total, fpad), F32)),
        grid=(rows_total // tm,),
        in_specs=[row_spec(dm), _const_spec((1, dm)), _const_spec(w.shape)],
        out_specs=(row_spec(dm),
                   pl.BlockSpec((1, npair, 1, V7X_LANES, tm), lambda i: (i // nk, 0, i % nk, 0, 0)),
                   row_spec(fpad)),
        compiler_params=_params(), name="kv_proj",
    )(h, g, w)


LOG2E = 1.4426950408889634
N_SPLIT = 3


def _cumsum_kernel(f_ref, b_ref, c_ref, *part_refs):
    x = jax.nn.log_sigmoid(f_ref[...] + b_ref[...])
    n = x.shape[1]
    lane = lax.broadcasted_iota(jnp.int32, x.shape, 1)
    shift = 1
    while shift < n:
        x = x + jnp.where(lane >= shift, pltpu.roll(x, shift, axis=1), 0.0)
        shift *= 2
    c = x * LOG2E
    c_ref[...] = c
    rest = -c
    for ref in part_refs:
        piece = rest.astype(BF16)
        ref[...] = piece
        rest = rest - piece.astype(F32)


def _forget_cumsum(f_t, b_col):
    return pl.pallas_call(
        _cumsum_kernel,
        out_shape=(jax.ShapeDtypeStruct(f_t.shape, F32),) + (jax.ShapeDtypeStruct(f_t.shape, BF16),) * N_SPLIT,
        name="forget_cumsum")(f_t, b_col)


ONES_ROWS = 16
PAIRS_PER_STEP = 2


def _attn_kernel(q_ref, k_ref, kaug_ref, vt_ref, cq_ref, o_ref, z0_ref, z1_ref, acc_ref, ws_ref, *, hd):
    npp, nq, tq = cq_ref.shape[1], cq_ref.shape[2], cq_ref.shape[4]
    tk = vt_ref.shape[4]
    heads = tuple((pp, h2) for pp in range(npp) for h2 in range(2))
    keep = (lax.broadcasted_iota(jnp.int32, (tk, tq), 0)
            <= lax.broadcasted_iota(jnp.int32, (tk, tq), 1))

    def tile_rows(t):
        return pl.ds(pl.multiple_of(t * tq, tq), tq)

    def pair_lanes(pp):
        return slice(pp * V7X_LANES, (pp + 1) * V7X_LANES)

    def prep_tile(t):
        for pp in range(npp):
            qt = q_ref[0, tile_rows(t), pair_lanes(pp)].astype(F32).T
            row = lax.broadcasted_iota(jnp.int32, qt.shape, 0)
            for h2 in range(2):
                qm = jnp.where((row >= h2 * hd) & (row < (h2 + 1) * hd), qt, 0.0)
                aw = jnp.where((row >= N_SPLIT * h2) & (row < N_SPLIT * (h2 + 1)), 1.0, 0.0)
                ws_ref[2 * pp + h2] = jnp.concatenate([qm, aw], axis=0).astype(BF16)

    def key_rows(j):
        rows = pl.ds(pl.multiple_of(j * tk, tk), tk)
        return tuple(jnp.concatenate([k_ref[0, rows, pair_lanes(pp)], kaug_ref[0, pp, rows, :]], axis=1)
                     for pp in range(npp))

    def scores_head(z_ref, kx, c):
        z = jnp.dot(kx[heads[c][0]], ws_ref[c], preferred_element_type=F32)
        z_ref[c] = z
        return jnp.max(z, axis=0, keepdims=True)

    def consume_head(z_ref, j, t, c, zm, m, masked):
        pp, h2 = heads[c]
        z, c_row = z_ref[c], cq_ref[0, pp, t, h2:h2 + 1, :]
        if masked:
            z = jnp.where(keep, z, -jnp.inf)
            zm = jnp.max(z, axis=0, keepdims=True)
        m_new = jnp.maximum(m, zm + c_row)
        alpha = jnp.exp2(m - m_new)
        p = jnp.exp2(z - (m_new - c_row)).astype(BF16)
        v_ext = jnp.concatenate([vt_ref[0, pp, j, h2 * hd:(h2 + 1) * hd, :],
                                 jnp.ones((ONES_ROWS, tk), BF16)], axis=0)
        acc_ref[c] = alpha * acc_ref[c] + jnp.dot(v_ext, p, preferred_element_type=F32)
        return m_new

    zbufs = (z0_ref, z1_ref)
    fresh_stats = tuple(jnp.full((1, tq), NEG_BIG, F32) for _ in heads)

    def inner_body(t, j, zmax, stats, pending):
        kx = key_rows(j + 1)
        new_zmax, new_stats = [], []
        for c in range(len(heads)):
            new_zmax.append(scores_head(zbufs[1 - pending], kx, c))
            new_stats.append(consume_head(zbufs[pending], j, t, c, zmax[c], stats[c], False))
        return tuple(new_zmax), tuple(new_stats)

    def last_body(t, stats, pending):
        prep_tile(jnp.minimum(t + 1, nq - 1))
        kx = key_rows(0)
        new_zmax = []
        for c in range(len(heads)):
            new_zmax.append(scores_head(zbufs[1 - pending], kx, c))
            consume_head(zbufs[pending], t, t, c, None, stats[c], True)
        for pp in range(npp):
            out_t = jnp.concatenate([acc_ref[2 * pp + h2, 0:hd, :] / acc_ref[2 * pp + h2, hd:hd + 1, :]
                                     for h2 in range(2)], axis=0)
            o_ref[0, tile_rows(t), pair_lanes(pp)] = out_t.T.astype(o_ref.dtype)
        acc_ref[...] = jnp.zeros_like(acc_ref)
        return tuple(new_zmax)

    def by_parity(fn, pending, *args):
        return lax.cond(pending == 0, functools.partial(fn, pending=0), functools.partial(fn, pending=1), *args)

    def tile_body(t, carry):
        zmax, pending = carry

        def step(j, inner):
            zmax, stats, pending = inner
            zmax, stats = by_parity(inner_body, pending, t, j, zmax, stats)
            return zmax, stats, 1 - pending

        zmax, stats, pending = lax.fori_loop(0, t, step, (zmax, fresh_stats, pending))
        return by_parity(last_body, pending, t, stats), 1 - pending

    prep_tile(0)
    acc_ref[...] = jnp.zeros_like(acc_ref)
    kx0 = key_rows(0)
    zmax0 = tuple(scores_head(z0_ref, kx0, c) for c in range(len(heads)))
    lax.fori_loop(0, nq, tile_body, (zmax0, jnp.int32(0)))


def _attention(q, k, kaug, vt, cq, *, hd):
    bsz, seq, dm = q.shape
    npair, nk, tq = vt.shape[1], vt.shape[2], vt.shape[4]
    npp = PAIRS_PER_STEP if npair % PAIRS_PER_STEP == 0 else 1
    nch = 2 * npp
    lane_spec = pl.BlockSpec((1, seq, npp * V7X_LANES), lambda b, p: (b, 0, p))
    return pl.pallas_call(
        functools.partial(_attn_kernel, hd=hd),
        out_shape=jax.ShapeDtypeStruct(q.shape, BF16),
        grid=(bsz, npair // npp),
        in_specs=[lane_spec, lane_spec,
                  pl.BlockSpec((1, npp, seq, V7X_LANES), lambda b, p: (b, p, 0, 0)),
                  pl.BlockSpec((1, npp, nk, V7X_LANES, tq), lambda b, p: (b, p, 0, 0, 0)),
                  pl.BlockSpec((1, npp, nk, 2, tq), lambda b, p: (b, p, 0, 0, 0))],
        out_specs=lane_spec,
        scratch_shapes=[pltpu.VMEM((nch, tq, tq), F32), pltpu.VMEM((nch, tq, tq), F32),
                        pltpu.VMEM((nch, hd + ONES_ROWS, tq), F32),
                        pltpu.VMEM((nch, 2 * V7X_LANES, tq), BF16)],
        compiler_params=_params(2), name="fox_attention",
    )(q, k, kaug, vt, cq)


def kernel(x, g_mix, g_ffn, lam_re, lam_im, log_dt, ssm_b_re, ssm_b_im, ssm_c_re, ssm_c_im, ssm_d,
           w_glu, g_kv, w_kvf, b_f, w_q, w_o, w_ffn_in, ffn_conv_w, ffn_conv_b, w_ffn_out, g_final):
    bsz, seq, dm = x.shape
    depth = g_mix.shape[0]
    n_a = lam_re.shape[0]
    nh = b_f.shape[0]
    hd = dm // nh
    ff = w_ffn_out.shape[1]
    assert bsz == V7X_SUBLANES and 2 * hd == V7X_LANES

    tm = min(512, seq)
    steps = tm // bsz
    fc = 256 if ff % 256 == 0 else V7X_LANES
    tq = min(512, seq)
    row = lambda a: a.reshape(1, -1)

    w_in_b = w_ffn_in.astype(BF16)
    w_out_b = w_ffn_out.astype(BF16)

    def ffn(h, layer, shift, seq_tiles, **fused):
        return _ffn_layer(h, row(g_ffn[layer]), w_in_b[layer], ffn_conv_w[layer], row(ffn_conv_b[layer]),
                          w_out_b[layer], tm=tm, shift=shift, seq_tiles=seq_tiles, fc=fc, **fused)

    h = x
    for layer in range(n_a):
        lb_re, lb_im, bb_re, bb_im = _s5_prep(lam_re[layer], lam_im[layer], log_dt[layer],
                                              ssm_b_re[layer], ssm_b_im[layer])
        h = _s5_layer(h, row(g_mix[layer]), _block_diag_in(bb_re, bb_im).astype(BF16),
                      _block_diag_out(ssm_c_re[layer]).astype(BF16),
                      _block_diag_out(ssm_c_im[layer]).astype(BF16),
                      row(lb_re), row(lb_im), row(ssm_d[layer]), w_glu[layer].astype(BF16),
                      bsz=bsz, steps=steps)
        h = ffn(h, layer, bsz, seq * bsz // tm, batch_major_out=bsz if layer == n_a - 1 else None)

    h = h.reshape(bsz * seq, dm)
    fpad = V7X_LANES
    w_kvf_b = jnp.pad(w_kvf, ((0, 0), (0, fpad - nh))).astype(BF16)
    k, vt, f_logit = _kv_proj(h, row(g_kv), w_kvf_b, bsz=bsz, tm=tq)
    f_t = f_logit[:, :nh].reshape(bsz, seq, nh).transpose(0, 2, 1).reshape(bsz * nh, seq)
    cum_t, *parts = _forget_cumsum(f_t, jnp.tile(b_f, bsz).reshape(bsz * nh, 1))
    npair = nh // 2
    nk = seq // tq
    cq = cum_t.reshape(bsz, npair, 2, nk, tq).transpose(0, 1, 3, 2, 4)
    kaug = jnp.stack(parts, axis=1).reshape(bsz, npair, 2 * N_SPLIT, seq).transpose(0, 1, 3, 2)
    kaug = jnp.pad(kaug, ((0, 0), (0, 0), (0, 0), (0, V7X_LANES - 2 * N_SPLIT)))
    k3 = k.reshape(bsz, seq, dm)
    for layer in range(n_a, depth):
        j = layer - n_a
        (q,) = _norm_proj(h, row(g_mix[layer]), w_q[j].astype(BF16), (dm,), (BF16,), tm=tm,
                          scale=hd ** -0.5 * LOG2E)
        o = _attention(q.reshape(bsz, seq, dm), k3, kaug, vt, cq, hd=hd)
        h = ffn(h, layer, 1, seq // tm, attn_out=o.reshape(bsz * seq, dm), w_o=w_o[j].astype(BF16),
                g_final=row(g_final) if layer == depth - 1 else None)
    return h.reshape(bsz, seq, dm)
```

```python
import functools

import jax
import jax.numpy as jnp
from jax import lax
from jax.experimental import pallas as pl
from jax.experimental.pallas import tpu as pltpu

EPS = 1e-6
V7X_LANES = 128
V7X_SUBLANES = 8
V7X_VMEM_LIMIT_BYTES = 56 * 1024 * 1024
F32 = jnp.float32
BF16 = jnp.bfloat16
NEG_BIG = -1e30


def _params(n_axes=1):
    return pltpu.CompilerParams(dimension_semantics=("arbitrary",) * n_axes,
                                vmem_limit_bytes=V7X_VMEM_LIMIT_BYTES)


def _const_spec(shape):
    nd = len(shape)
    return pl.BlockSpec(shape, lambda *_: (0,) * nd, pipeline_mode=pl.Buffered(1))


def _rms(x, g):
    return x * lax.rsqrt(jnp.mean(x * x, axis=-1, keepdims=True) + EPS) * g


def _s5_prep_kernel(lr_ref, li_ref, ldt_ref, br_ref, bi_ref, lbr_ref, lbi_ref, bbr_ref, bbi_ref):
    lr = lr_ref[...]
    li = li_ref[...]
    dt = jnp.exp(ldt_ref[...])
    mag = jnp.exp(lr * dt)
    lb_re = mag * jnp.cos(li * dt)
    lb_im = mag * jnp.sin(li * dt)
    den = lr * lr + li * li
    nr = lb_re - 1.0
    fr = (nr * lr + lb_im * li) / den
    fi = (lb_im * lr - nr * li) / den
    br = br_ref[...]
    bi = bi_ref[...]
    lbr_ref[...] = lb_re
    lbi_ref[...] = lb_im
    bbr_ref[...] = fr * br - fi * bi
    bbi_ref[...] = fr * bi + fi * br


def _s5_prep(lam_re, lam_im, log_dt, b_re, b_im):
    g, p, h = b_re.shape
    rep = lambda a: jnp.repeat(a, h, axis=0)
    args = (rep(lam_re), rep(lam_im), rep(jnp.broadcast_to(log_dt[:, None], (g, p))),
            b_re.transpose(0, 2, 1).reshape(g * h, p), b_im.transpose(0, 2, 1).reshape(g * h, p))
    shp = jax.ShapeDtypeStruct((g * h, p), F32)
    lbr, lbi, bbr, bbi = pl.pallas_call(_s5_prep_kernel, out_shape=(shp,) * 4, name="s5_prep")(*args)
    lb_re = lbr.reshape(g, h, p)[:, 0, :]
    lb_im = lbi.reshape(g, h, p)[:, 0, :]
    return lb_re, lb_im, bbr.reshape(g, h, p), bbi.reshape(g, h, p)


def _block_diag_in(bb_re, bb_im):
    g, h, p = bb_re.shape
    gpb = V7X_LANES // h
    nb = g // gpb
    eye = jnp.eye(gpb, dtype=F32)
    bb = jnp.stack([bb_re, bb_im]).reshape(2, nb, gpb, h, p)
    return jnp.einsum("rkghp,gj->kghrjp", bb, eye).reshape(nb, gpb * h, 2 * gpb * p)


def _block_diag_out(c):
    g, h, p = c.shape
    gpb = V7X_LANES // h
    nb = g // gpb
    eye = jnp.eye(gpb, dtype=F32)
    return jnp.einsum("kghp,gj->kjpgh", c.reshape(nb, gpb, h, p), eye).reshape(nb, gpb * p, gpb * h)


def _s5_kernel(h_ref, g_ref, bblk_ref, cre_ref, cim_ref, are_ref, aim_ref, d_ref, wglu_ref,
               o_ref, bu_ref, st_ref, *, nb, bsz, steps, unroll):
    half = are_ref.shape[1] // nb
    width = 2 * half

    @pl.when(pl.program_id(0) == 0)
    def _():
        st_ref[...] = jnp.zeros_like(st_ref)

    x = h_ref[...]
    if x.ndim == 3:
        x = jnp.swapaxes(x, 0, 1).reshape(bsz * steps, x.shape[2])
    u = _rms(x, g_ref[...])
    ub = u.astype(BF16)
    for k in range(nb):
        bu_ref[:, k * width:(k + 1) * width] = jnp.dot(
            ub[:, k * V7X_LANES:(k + 1) * V7X_LANES], bblk_ref[k], preferred_element_type=F32)

    for k in range(nb):
        re_cols = slice(k * width, k * width + half)
        im_cols = slice(k * width + half, (k + 1) * width)
        a_re = jnp.broadcast_to(are_ref[:, k * half:(k + 1) * half], (bsz, half))
        a_im = jnp.broadcast_to(aim_ref[:, k * half:(k + 1) * half], (bsz, half))

        def step(t, carry, re_cols=re_cols, im_cols=im_cols, a_re=a_re, a_im=a_im):
            s_re, s_im = carry
            rows = pl.ds(pl.multiple_of(t * bsz, bsz), bsz)
            n_re = a_re * s_re - a_im * s_im + bu_ref[rows, re_cols]
            n_im = a_re * s_im + a_im * s_re + bu_ref[rows, im_cols]
            bu_ref[rows, re_cols] = n_re
            bu_ref[rows, im_cols] = n_im
            return n_re, n_im

        s_re, s_im = lax.fori_loop(0, steps, step, (st_ref[:, re_cols], st_ref[:, im_cols]),
                                   unroll=unroll)
        st_ref[:, re_cols] = s_re
        st_ref[:, im_cols] = s_im

    ys = []
    for k in range(nb):
        s_re = bu_ref[:, k * width:k * width + half].astype(BF16)
        s_im = bu_ref[:, k * width + half:(k + 1) * width].astype(BF16)
        ys.append(jnp.dot(s_re, cre_ref[k], preferred_element_type=F32)
                  - jnp.dot(s_im, cim_ref[k], preferred_element_type=F32))
    y = jnp.concatenate(ys, axis=-1) + d_ref[...] * u
    y = jax.nn.gelu(y)
    z = jnp.dot(y.astype(BF16), wglu_ref[...], preferred_element_type=F32)
    dm = x.shape[-1]
    o_ref[...] = x + z[:, :dm] * jax.nn.sigmoid(z[:, dm:])


def _s5_layer(h, g, bblk, cre, cim, a_re, a_im, d_skip, w_glu, *, bsz, steps):
    dm = h.shape[-1]
    rows_total = h.size // dm
    rows = bsz * steps
    nb = bblk.shape[0]
    nstate = a_re.shape[1]
    kern = functools.partial(_s5_kernel, nb=nb, bsz=bsz, steps=steps, unroll=True)
    h_spec = (pl.BlockSpec((rows, dm), lambda i: (i, 0)) if h.ndim == 2
              else pl.BlockSpec((bsz, steps, dm), lambda i: (0, i, 0)))
    return pl.pallas_call(
        kern,
        out_shape=jax.ShapeDtypeStruct((rows_total, dm), F32),
        grid=(rows_total // rows,),
        in_specs=[h_spec,
                  _const_spec((1, dm)), _const_spec(bblk.shape), _const_spec(cre.shape),
                  _const_spec(cim.shape), _const_spec(a_re.shape), _const_spec(a_im.shape),
                  _const_spec((1, dm)), _const_spec(w_glu.shape)],
        out_specs=pl.BlockSpec((rows, dm), lambda i: (i, 0)),
        scratch_shapes=[pltpu.VMEM((rows, 2 * nstate), F32), pltpu.VMEM((bsz, 2 * nstate), F32)],
        compiler_params=_params(), name="s5_layer",
    )(h, g, bblk, cre, cim, a_re, a_im, d_skip, w_glu)


def _ffn_kernel(*refs, shift, halo, seq_tiles, fc, pre_proj, post_norm):
    refs = list(refs)
    h_ref = refs.pop(0)
    if pre_proj:
        oin_ref, wo_ref = refs.pop(0), refs.pop(0)
    g_ref, win_ref, cw_ref, cb_ref, wout_ref = (refs.pop(0) for _ in range(5))
    if post_norm:
        gf_ref = refs.pop(0)
    o_ref, ubuf_ref, halo_ref, act_ref = refs
    tm = h_ref.shape[0]
    ff = act_ref.shape[1]
    x = h_ref[...]
    if pre_proj:
        x = x + jnp.dot(oin_ref[...], wo_ref[...], preferred_element_type=F32)
    hb = _rms(x, g_ref[...]).astype(BF16)

    @pl.when(pl.program_id(0) % seq_tiles == 0)
    def _():
        halo_ref[...] = jnp.zeros_like(halo_ref)

    def conv(u, cols, buf_cols):
        ubuf_ref[0:halo, buf_cols] = halo_ref[:, cols]
        ubuf_ref[halo:halo + tm, buf_cols] = u
        halo_ref[:, cols] = u[tm - halo:, :]
        return (cb_ref[:, cols] + cw_ref[2:3, cols] * u
                + cw_ref[1:2, cols] * ubuf_ref[halo - shift:halo - shift + tm, buf_cols]
                + cw_ref[0:1, cols] * ubuf_ref[halo - 2 * shift:halo - 2 * shift + tm, buf_cols])

    for c in range(ff // fc):
        gate_cols = slice(c * fc, (c + 1) * fc)
        up_cols = slice(ff + c * fc, ff + (c + 1) * fc)
        gate = conv(jnp.dot(hb, win_ref[:, gate_cols], preferred_element_type=F32), gate_cols, slice(0, fc))
        up = conv(jnp.dot(hb, win_ref[:, up_cols], preferred_element_type=F32), up_cols, slice(fc, 2 * fc))
        act_ref[:, gate_cols] = (jax.nn.silu(gate) * up).astype(BF16)

    out = x + jnp.dot(act_ref[...], wout_ref[...], preferred_element_type=F32)
    if post_norm:
        out = _rms(out, gf_ref[...])
    if len(o_ref.shape) == 3:
        nb = o_ref.shape[0]
        out = jnp.swapaxes(out.reshape(tm // nb, nb, out.shape[1]), 0, 1)
    o_ref[...] = out


def _ffn_layer(h, g, w_in, conv_w, conv_b, w_out, *, tm, shift, seq_tiles, fc, attn_out=None, w_o=None,
               g_final=None, batch_major_out=None):
    rows_total, dm = h.shape
    ff = w_out.shape[0]
    halo = max(V7X_SUBLANES, 2 * shift)
    row_spec = pl.BlockSpec((tm, dm), lambda i: (i, 0))
    if batch_major_out is None:
        out_shape, out_spec = (rows_total, dm), row_spec
    else:
        out_shape = (batch_major_out, rows_total // batch_major_out, dm)
        out_spec = pl.BlockSpec((batch_major_out, tm // batch_major_out, dm), lambda i: (0, i, 0))
    pre_proj, post_norm = attn_out is not None, g_final is not None
    args, specs = [h], [row_spec]
    if pre_proj:
        args += [attn_out, w_o]
        specs += [row_spec, _const_spec(w_o.shape)]
    args += [g, w_in, conv_w, conv_b, w_out]
    specs += [_const_spec((1, dm)), _const_spec(w_in.shape), _const_spec(conv_w.shape),
              _const_spec(conv_b.shape), _const_spec(w_out.shape)]
    if post_norm:
        args.append(g_final)
        specs.append(_const_spec((1, dm)))
    kern = functools.partial(_ffn_kernel, shift=shift, halo=halo, seq_tiles=seq_tiles, fc=fc,
                             pre_proj=pre_proj, post_norm=post_norm)
    return pl.pallas_call(
        kern,
        out_shape=jax.ShapeDtypeStruct(out_shape, F32),
        grid=(rows_total // tm,),
        in_specs=specs,
        out_specs=out_spec,
        scratch_shapes=[pltpu.VMEM((tm + halo, 2 * fc), F32), pltpu.VMEM((halo, 2 * ff), F32),
                        pltpu.VMEM((tm, ff), BF16)],
        compiler_params=_params(), name="conv_ffn",
    )(*args)


def _norm_proj_kernel(h_ref, g_ref, w_ref, *o_refs, scale):
    hb = _rms(h_ref[...], g_ref[...]).astype(BF16)
    z = jnp.dot(hb, w_ref[...], preferred_element_type=F32)
    col = 0
    for o_ref in o_refs:
        n = o_ref.shape[1]
        o_ref[...] = (z[:, col:col + n] * scale).astype(o_ref.dtype)
        col += n


def _norm_proj(h, g, w, out_widths, out_dtypes, *, tm, scale=1.0):
    rows_total, dm = h.shape
    return pl.pallas_call(
        functools.partial(_norm_proj_kernel, scale=scale),
        out_shape=tuple(jax.ShapeDtypeStruct((rows_total, n), dt) for n, dt in zip(out_widths, out_dtypes)),
        grid=(rows_total // tm,),
        in_specs=[pl.BlockSpec((tm, dm), lambda i: (i, 0)), _const_spec((1, dm)), _const_spec(w.shape)],
        out_specs=tuple(pl.BlockSpec((tm, n), lambda i: (i, 0)) for n in out_widths),
        compiler_params=_params(), name="norm_proj",
    )(h, g, w)


def _kv_proj_kernel(h_ref, g_ref, w_ref, k_ref, vt_ref, f_ref):
    dm = k_ref.shape[1]
    hb = _rms(h_ref[...], g_ref[...]).astype(BF16)
    z = jnp.dot(hb, w_ref[...], preferred_element_type=F32)
    k_ref[...] = z[:, :dm].astype(k_ref.dtype)
    for p in range(vt_ref.shape[1]):
        lanes = slice(dm + p * V7X_LANES, dm + (p + 1) * V7X_LANES)
        vt_ref[0, p, 0] = z[:, lanes].T.astype(vt_ref.dtype)
    f_ref[...] = z[:, 2 * dm:]


def _kv_proj(h, g, w, *, bsz, tm):
    rows_total, dm = h.shape
    npair = dm // V7X_LANES
    nk = rows_total // bsz // tm
    fpad = w.shape[1] - 2 * dm
    row_spec = lambda n: pl.BlockSpec((tm, n), lambda i: (i, 0))
    return pl.pallas_call(
        _kv_proj_kernel,
        out_shape=(jax.ShapeDtypeStruct((rows_total, dm), BF16),
                   jax.ShapeDtypeStruct((bsz, npair, nk, V7X_LANES, tm), BF16),
                   jax.ShapeDtypeStruct((rows_total, fpad), F32)),
        grid=(rows_total // tm,),
        in_specs=[row_spec(dm), _const_spec((1, dm)), _const_spec(w.shape)],
        out_specs=(row_spec(dm),
                   pl.BlockSpec((1, npair, 1, V7X_LANES, tm), lambda i: (i // nk, 0, i % nk, 0, 0)),
                   row_spec(fpad)),
        compiler_params=_params(), name="kv_proj",
    )(h, g, w)


LOG2E = 1.4426950408889634
N_SPLIT = 3


def _cumsum_kernel(f_ref, b_ref, c_ref, *part_refs):
    x = jax.nn.log_sigmoid(f_ref[...] + b_ref[...])
    n = x.shape[1]
    lane = lax.broadcasted_iota(jnp.int32, x.shape, 1)
    shift = 1
    while shift < n:
        x = x + jnp.where(lane >= shift, pltpu.roll(x, shift, axis=1), 0.0)
        shift *= 2
    c = x * LOG2E
    c_ref[...] = c
    rest = -c
    for ref in part_refs:
        piece = rest.astype(BF16)
        ref[...] = piece
        rest = rest - piece.astype(F32)


def _forget_cumsum(f_t, b_col):
    return pl.pallas_call(
        _cumsum_kernel,
        out_shape=(jax.ShapeDtypeStruct(f_t.shape, F32),) + (jax.ShapeDtypeStruct(f_t.shape, BF16),) * N_SPLIT,
        name="forget_cumsum")(f_t, b_col)


ONES_ROWS = 16
PAIRS_PER_STEP = 2


def _attn_kernel(q_ref, k_ref, kaug_ref, vt_ref, cq_ref, o_ref, z0_ref, z1_ref, acc_ref, ws_ref, *, hd):
    npp, nq, tq = cq_ref.shape[1], cq_ref.shape[2], cq_ref.shape[4]
    tk = vt_ref.shape[4]
    heads = tuple((pp, h2) for pp in range(npp) for h2 in range(2))
    keep = (lax.broadcasted_iota(jnp.int32, (tk, tq), 0)
            <= lax.broadcasted_iota(jnp.int32, (tk, tq), 1))

    def tile_rows(t):
        return pl.ds(pl.multiple_of(t * tq, tq), tq)

    def pair_lanes(pp):
        return slice(pp * V7X_LANES, (pp + 1) * V7X_LANES)

    def prep_tile(t):
        for pp in range(npp):
            qt = q_ref[0, tile_rows(t), pair_lanes(pp)].astype(F32).T
            row = lax.broadcasted_iota(jnp.int32, qt.shape, 0)
            for h2 in range(2):
                qm = jnp.where((row >= h2 * hd) & (row < (h2 + 1) * hd), qt, 0.0)
                aw = jnp.where((row >= N_SPLIT * h2) & (row < N_SPLIT * (h2 + 1)), 1.0, 0.0)
                ws_ref[2 * pp + h2] = jnp.concatenate([qm, aw], axis=0).astype(BF16)

    def key_rows(j):
        rows = pl.ds(pl.multiple_of(j * tk, tk), tk)
        return tuple(jnp.concatenate([k_ref[0, rows, pair_lanes(pp)], kaug_ref[0, pp, rows, :]], axis=1)
                     for pp in range(npp))

    def scores_head(z_ref, kx, c):
        z = jnp.dot(kx[heads[c][0]], ws_ref[c], preferred_element_type=F32)
        z_ref[c] = z
        return jnp.max(z, axis=0, keepdims=True)

    def consume_head(z_ref, j, t, c, zm, m, masked):
        pp, h2 = heads[c]
        z, c_row = z_ref[c], cq_ref[0, pp, t, h2:h2 + 1, :]
        if masked:
            z = jnp.where(keep, z, -jnp.inf)
            zm = jnp.max(z, axis=0, keepdims=True)
        m_new = jnp.maximum(m, zm + c_row)
        alpha = jnp.exp2(m - m_new)
        p = jnp.exp2(z - (m_new - c_row)).astype(BF16)
        v_ext = jnp.concatenate([vt_ref[0, pp, j, h2 * hd:(h2 + 1) * hd, :],
                                 jnp.ones((ONES_ROWS, tk), BF16)], axis=0)
        acc_ref[c] = alpha * acc_ref[c] + jnp.dot(v_ext, p, preferred_element_type=F32)
        return m_new

    zbufs = (z0_ref, z1_ref)
    fresh_stats = tuple(jnp.full((1, tq), NEG_BIG, F32) for _ in heads)

    def inner_body(t, j, zmax, stats, pending):
        kx = key_rows(j + 1)
        new_zmax, new_stats = [], []
        for c in range(len(heads)):
            new_zmax.append(scores_head(zbufs[1 - pending], kx, c))
            new_stats.append(consume_head(zbufs[pending], j, t, c, zmax[c], stats[c], False))
        return tuple(new_zmax), tuple(new_stats)

    def last_body(t, stats, pending):
        prep_tile(jnp.minimum(t + 1, nq - 1))
        kx = key_rows(0)
        new_zmax = []
        for c in range(len(heads)):
            new_zmax.append(scores_head(zbufs[1 - pending], kx, c))
            consume_head(zbufs[pending], t, t, c, None, stats[c], True)
        for pp in range(npp):
            out_t = jnp.concatenate([acc_ref[2 * pp + h2, 0:hd, :] / acc_ref[2 * pp + h2, hd:hd + 1, :]
                                     for h2 in range(2)], axis=0)
            o_ref[0, tile_rows(t), pair_lanes(pp)] = out_t.T.astype(o_ref.dtype)
        acc_ref[...] = jnp.zeros_like(acc_ref)
        return tuple(new_zmax)

    def by_parity(fn, pending, *args):
        return lax.cond(pending == 0, functools.partial(fn, pending=0), functools.partial(fn, pending=1), *args)

    def tile_body(t, carry):
        zmax, pending = carry

        def step(j, inner):
            zmax, stats, pending = inner
            zmax, stats = by_parity(inner_body, pending, t, j, zmax, stats)
            return zmax, stats, 1 - pending

        zmax, stats, pending = lax.fori_loop(0, t, step, (zmax, fresh_stats, pending))
        return by_parity(last_body, pending, t, stats), 1 - pending

    prep_tile(0)
    acc_ref[...] = jnp.zeros_like(acc_ref)
    kx0 = key_rows(0)
    zmax0 = tuple(scores_head(z0_ref, kx0, c) for c in range(len(heads)))
    lax.fori_loop(0, nq, tile_body, (zmax0, jnp.int32(0)))


def _attention(q, k, kaug, vt, cq, *, hd):
    bsz, seq, dm = q.shape
    npair, nk, tq = vt.shape[1], vt.shape[2], vt.shape[4]
    npp = PAIRS_PER_STEP if npair % PAIRS_PER_STEP == 0 else 1
    nch = 2 * npp
    lane_spec = pl.BlockSpec((1, seq, npp * V7X_LANES), lambda b, p: (b, 0, p))
    return pl.pallas_call(
        functools.partial(_attn_kernel, hd=hd),
        out_shape=jax.ShapeDtypeStruct(q.shape, BF16),
        grid=(bsz, npair // npp),
        in_specs=[lane_spec, lane_spec,
                  pl.BlockSpec((1, npp, seq, V7X_LANES), lambda b, p: (b, p, 0, 0)),
                  pl.BlockSpec((1, npp, nk, V7X_LANES, tq), lambda b, p: (b, p, 0, 0, 0)),
                  pl.BlockSpec((1, npp, nk, 2, tq), lambda b, p: (b, p, 0, 0, 0))],
        out_specs=lane_spec,
        scratch_shapes=[pltpu.VMEM((nch, tq, tq), F32), pltpu.VMEM((nch, tq, tq), F32),
                        pltpu.VMEM((nch, hd + ONES_ROWS, tq), F32),
                        pltpu.VMEM((nch, 2 * V7X_LANES, tq), BF16)],
        compiler_params=_params(2), name="fox_attention",
    )(q, k, kaug, vt, cq)


def kernel(x, g_mix, g_ffn, lam_re, lam_im, log_dt, ssm_b_re, ssm_b_im, ssm_c_re, ssm_c_im, ssm_d,
           w_glu, g_kv, w_kvf, b_f, w_q, w_o, w_ffn_in, ffn_conv_w, ffn_conv_b, w_ffn_out, g_final):
    bsz, seq, dm = x.shape
    depth = g_mix.shape[0]
    n_a = lam_re.shape[0]
    nh = b_f.shape[0]
    hd = dm // nh
    ff = w_ffn_out.shape[1]
    assert bsz == V7X_SUBLANES and 2 * hd == V7X_LANES

    tm = min(512, seq)
    steps = tm // bsz
    fc = 256 if ff % 256 == 0 else V7X_LANES
    tq = min(512, seq)
    row = lambda a: a.reshape(1, -1)

    w_in_b = w_ffn_in.astype(BF16)
    w_out_b = w_ffn_out.astype(BF16)

    def ffn(h, layer, shift, seq_tiles, **fused):
        return _ffn_layer(h, row(g_ffn[layer]), w_in_b[layer], ffn_conv_w[layer], row(ffn_conv_b[layer]),
                          w_out_b[layer], tm=tm, shift=shift, seq_tiles=seq_tiles, fc=fc, **fused)

    h = x
    for layer in range(n_a):
        lb_re, lb_im, bb_re, bb_im = _s5_prep(lam_re[layer], lam_im[layer], log_dt[layer],
                                              ssm_b_re[layer], ssm_b_im[layer])
        h = _s5_layer(h, row(g_mix[layer]), _block_diag_in(bb_re, bb_im).astype(BF16),
                      _block_diag_out(ssm_c_re[layer]).astype(BF16),
                      _block_diag_out(ssm_c_im[layer]).astype(BF16),
                      row(lb_re), row(lb_im), row(ssm_d[layer]), w_glu[layer].astype(BF16),
                      bsz=bsz, steps=steps)
        h = ffn(h, layer, bsz, seq * bsz // tm, batch_major_out=bsz if layer == n_a - 1 else None)

    h = h.reshape(bsz * seq, dm)
    fpad = V7X_LANES
    w_kvf_b = jnp.pad(w_kvf, ((0, 0), (0, fpad - nh))).astype(BF16)
    k, vt, f_logit = _kv_proj(h, row(g_kv), w_kvf_b, bsz=bsz, tm=tq)
    f_t = f_logit[:, :nh].reshape(bsz, seq, nh).transpose(0, 2, 1).reshape(bsz * nh, seq)
    cum_t, *parts = _forget_cumsum(f_t, jnp.tile(b_f, bsz).reshape(bsz * nh, 1))
    npair = nh // 2
    nk = seq // tq
    cq = cum_t.reshape(bsz, npair, 2, nk, tq).transpose(0, 1, 3, 2, 4)
    kaug = jnp.stack(parts, axis=1).reshape(bsz, npair, 2 * N_SPLIT, seq).transpose(0, 1, 3, 2)
    kaug = jnp.pad(kaug, ((0, 0), (0, 0), (0, 0), (0, V7X_LANES - 2 * N_SPLIT)))
    k3 = k.reshape(bsz, seq, dm)
    for layer in range(n_a, depth):
        j = layer - n_a
        (q,) = _norm_proj(h, row(g_mix[layer]), w_q[j].astype(BF16), (dm,), (BF16,), tm=tm,
                          scale=hd ** -0.5 * LOG2E)
        o = _attention(q.reshape(bsz, seq, dm), k3, kaug, vt, cq, hd=hd)
        h = ffn(h, layer, 1, seq // tm, attn_out=o.reshape(bsz * seq, dm), w_o=w_o[j].astype(BF16),
                g_final=row(g_final) if layer == depth - 1 else None)
    return h.reshape(bsz, seq, dm)
```

```python
import functools

import jax
import jax.numpy as jnp
from jax import lax
from jax.experimental import pallas as pl
from jax.experimental.pallas import tpu as pltpu

EPS = 1e-6
V7X_LANES = 128
V7X_SUBLANES = 8
V7X_VMEM_LIMIT_BYTES = 56 * 1024 * 1024
F32 = jnp.float32
BF16 = jnp.bfloat16
NEG_BIG = -1e30


def _params(n_axes=1):
    return pltpu.CompilerParams(dimension_semantics=("arbitrary",) * n_axes,
                                vmem_limit_bytes=V7X_VMEM_LIMIT_BYTES)


def _const_spec(shape):
    nd = len(shape)
    return pl.BlockSpec(shape, lambda *_: (0,) * nd, pipeline_mode=pl.Buffered(1))


def _rms(x, g):
    return x * lax.rsqrt(jnp.mean(x * x, axis=-1, keepdims=True) + EPS) * g


def _s5_prep_kernel(lr_ref, li_ref, ldt_ref, br_ref, bi_ref, lbr_ref, lbi_ref, bbr_ref, bbi_ref):
    lr = lr_ref[...]
    li = li_ref[...]
    dt = jnp.exp(ldt_ref[...])
    mag = jnp.exp(lr * dt)
    lb_re = mag * jnp.cos(li * dt)
    lb_im = mag * jnp.sin(li * dt)
    den = lr * lr + li * li
    nr = lb_re - 1.0
    fr = (nr * lr + lb_im * li) / den
    fi = (lb_im * lr - nr * li) / den
    br = br_ref[...]
    bi = bi_ref[...]
    lbr_ref[...] = lb_re
    lbi_ref[...] = lb_im
    bbr_ref[...] = fr * br - fi * bi
    bbi_ref[...] = fr * bi + fi * br


def _s5_prep(lam_re, lam_im, log_dt, b_re, b_im):
    g, p, h = b_re.shape
    rep = lambda a: jnp.repeat(a, h, axis=0)
    args = (rep(lam_re), rep(lam_im), rep(jnp.broadcast_to(log_dt[:, None], (g, p))),
            b_re.transpose(0, 2, 1).reshape(g * h, p), b_im.transpose(0, 2, 1).reshape(g * h, p))
    shp = jax.ShapeDtypeStruct((g * h, p), F32)
    lbr, lbi, bbr, bbi = pl.pallas_call(_s5_prep_kernel, out_shape=(shp,) * 4, name="s5_prep")(*args)
    lb_re = lbr.reshape(g, h, p)[:, 0, :]
    lb_im = lbi.reshape(g, h, p)[:, 0, :]
    return lb_re, lb_im, bbr.reshape(g, h, p), bbi.reshape(g, h, p)


def _block_diag_in(bb_re, bb_im):
    g, h, p = bb_re.shape
    gpb = V7X_LANES // h
    nb = g // gpb
    eye = jnp.eye(gpb, dtype=F32)
    bb = jnp.stack([bb_re, bb_im]).reshape(2, nb, gpb, h, p)
    return jnp.einsum("rkghp,gj->kghrjp", bb, eye).reshape(nb, gpb * h, 2 * gpb * p)


def _block_diag_out(c):
    g, h, p = c.shape
    gpb = V7X_LANES // h
    nb = g // gpb
    eye = jnp.eye(gpb, dtype=F32)
    return jnp.einsum("kghp,gj->kjpgh", c.reshape(nb, gpb, h, p), eye).reshape(nb, gpb * p, gpb * h)


def _s5_kernel(h_ref, g_ref, bblk_ref, cre_ref, cim_ref, are_ref, aim_ref, d_ref, wglu_ref,
               o_ref, bu_ref, st_ref, *, nb, bsz, steps, unroll):
    half = are_ref.shape[1] // nb
    width = 2 * half

    @pl.when(pl.program_id(0) == 0)
    def _():
        st_ref[...] = jnp.zeros_like(st_ref)

    x = h_ref[...]
    if x.ndim == 3:
        x = jnp.swapaxes(x, 0, 1).reshape(bsz * steps, x.shape[2])
    u = _rms(x, g_ref[...])
    ub = u.astype(BF16)
    for k in range(nb):
        bu_ref[:, k * width:(k + 1) * width] = jnp.dot(
            ub[:, k * V7X_LANES:(k + 1) * V7X_LANES], bblk_ref[k], preferred_element_type=F32)

    for k in range(nb):
        re_cols = slice(k * width, k * width + half)
        im_cols = slice(k * width + half, (k + 1) * width)
        a_re = jnp.broadcast_to(are_ref[:, k * half:(k + 1) * half], (bsz, half))
        a_im = jnp.broadcast_to(aim_ref[:, k * half:(k + 1) * half], (bsz, half))

        def step(t, carry, re_cols=re_cols, im_cols=im_cols, a_re=a_re, a_im=a_im):
            s_re, s_im = carry
            rows = pl.ds(pl.multiple_of(t * bsz, bsz), bsz)
            n_re = a_re * s_re - a_im * s_im + bu_ref[rows, re_cols]
            n_im = a_re * s_im + a_im * s_re + bu_ref[rows, im_cols]
            bu_ref[rows, re_cols] = n_re
            bu_ref[rows, im_cols] = n_im
            return n_re, n_im

        s_re, s_im = lax.fori_loop(0, steps, step, (st_ref[:, re_cols], st_ref[:, im_cols]),
                                   unroll=unroll)
        st_ref[:, re_cols] = s_re
        st_ref[:, im_cols] = s_im

    ys = []
    for k in range(nb):
        s_re = bu_ref[:, k * width:k * width + half].astype(BF16)
        s_im = bu_ref[:, k * width + half:(k + 1) * width].astype(BF16)
        ys.append(jnp.dot(s_re, cre_ref[k], preferred_element_type=F32)
                  - jnp.dot(s_im, cim_ref[k], preferred_element_type=F32))
    y = jnp.concatenate(ys, axis=-1) + d_ref[...] * u
    y = jax.nn.gelu(y)
    z = jnp.dot(y.astype(BF16), wglu_ref[...], preferred_element_type=F32)
    dm = x.shape[-1]
    o_ref[...] = x + z[:, :dm] * jax.nn.sigmoid(z[:, dm:])


def _s5_layer(h, g, bblk, cre, cim, a_re, a_im, d_skip, w_glu, *, bsz, steps):
    dm = h.shape[-1]
    rows_total = h.size // dm
    rows = bsz * steps
    nb = bblk.shape[0]
    nstate = a_re.shape[1]
    kern = functools.partial(_s5_kernel, nb=nb, bsz=bsz, steps=steps, unroll=True)
    h_spec = (pl.BlockSpec((rows, dm), lambda i: (i, 0)) if h.ndim == 2
              else pl.BlockSpec((bsz, steps, dm), lambda i: (0, i, 0)))
    return pl.pallas_call(
        kern,
        out_shape=jax.ShapeDtypeStruct((rows_total, dm), F32),
        grid=(rows_total // rows,),
        in_specs=[h_spec,
                  _const_spec((1, dm)), _const_spec(bblk.shape), _const_spec(cre.shape),
                  _const_spec(cim.shape), _const_spec(a_re.shape), _const_spec(a_im.shape),
                  _const_spec((1, dm)), _const_spec(w_glu.shape)],
        out_specs=pl.BlockSpec((rows, dm), lambda i: (i, 0)),
        scratch_shapes=[pltpu.VMEM((rows, 2 * nstate), F32), pltpu.VMEM((bsz, 2 * nstate), F32)],
        compiler_params=_params(), name="s5_layer",
    )(h, g, bblk, cre, cim, a_re, a_im, d_skip, w_glu)


def _ffn_kernel(*refs, shift, halo, seq_tiles, fc, pre_proj, post_norm):
    refs = list(refs)
    h_ref = refs.pop(0)
    if pre_proj:
        oin_ref, wo_ref = refs.pop(0), refs.pop(0)
    g_ref, win_ref, cw_ref, cb_ref, wout_ref = (refs.pop(0) for _ in range(5))
    if post_norm:
        gf_ref = refs.pop(0)
    o_ref, ubuf_ref, halo_ref, act_ref = refs
    tm = h_ref.shape[0]
    ff = act_ref.shape[1]
    x = h_ref[...]
    if pre_proj:
        x = x + jnp.dot(oin_ref[...], wo_ref[...], preferred_element_type=F32)
    hb = _rms(x, g_ref[...]).astype(BF16)

    @pl.when(pl.program_id(0) % seq_tiles == 0)
    def _():
        halo_ref[...] = jnp.zeros_like(halo_ref)

    def conv(u, cols, buf_cols):
        ubuf_ref[0:halo, buf_cols] = halo_ref[:, cols]
        ubuf_ref[halo:halo + tm, buf_cols] = u
        halo_ref[:, cols] = u[tm - halo:, :]
        return (cb_ref[:, cols] + cw_ref[2:3, cols] * u
                + cw_ref[1:2, cols] * ubuf_ref[halo - shift:halo - shift + tm, buf_cols]
                + cw_ref[0:1, cols] * ubuf_ref[halo - 2 * shift:halo - 2 * shift + tm, buf_cols])

    for c in range(ff // fc):
        gate_cols = slice(c * fc, (c + 1) * fc)
        up_cols = slice(ff + c * fc, ff + (c + 1) * fc)
        gate = conv(jnp.dot(hb, win_ref[:, gate_cols], preferred_element_type=F32), gate_cols, slice(0, fc))
        up = conv(jnp.dot(hb, win_ref[:, up_cols], preferred_element_type=F32), up_cols, slice(fc, 2 * fc))
        act_ref[:, gate_cols] = (jax.nn.silu(gate) * up).astype(BF16)

    out = x + jnp.dot(act_ref[...], wout_ref[...], preferred_element_type=F32)
    if post_norm:
        out = _rms(out, gf_ref[...])
    if len(o_ref.shape) == 3:
        nb = o_ref.shape[0]
        out = jnp.swapaxes(out.reshape(tm // nb, nb, out.shape[1]), 0, 1)
    o_ref[...] = out


def _ffn_layer(h, g, w_in, conv_w, conv_b, w_out, *, tm, shift, seq_tiles, fc, attn_out=None, w_o=None,
               g_final=None, batch_major_out=None):
    rows_total, dm = h.shape
    ff = w_out.shape[0]
    halo = max(V7X_SUBLANES, 2 * shift)
    row_spec = pl.BlockSpec((tm, dm), lambda i: (i, 0))
    if batch_major_out is None:
        out_shape, out_spec = (rows_total, dm), row_spec
    else:
        out_shape = (batch_major_out, rows_total // batch_major_out, dm)
        out_spec = pl.BlockSpec((batch_major_out, tm // batch_major_out, dm), lambda i: (0, i, 0))
    pre_proj, post_norm = attn_out is not None, g_final is not None
    args, specs = [h], [row_spec]
    if pre_proj:
        args += [attn_out, w_o]
        specs += [row_spec, _const_spec(w_o.shape)]
    args += [g, w_in, conv_w, conv_b, w_out]
    specs += [_const_spec((1, dm)), _const_spec(w_in.shape), _const_spec(conv_w.shape),
              _const_spec(conv_b.shape), _const_spec(w_out.shape)]
    if post_norm:
        args.append(g_final)
        specs.append(_const_spec((1, dm)))
    kern = functools.partial(_ffn_kernel, shift=shift, halo=halo, seq_tiles=seq_tiles, fc=fc,
                             pre_proj=pre_proj, post_norm=post_norm)
    return pl.pallas_call(
        kern,
        out_shape=jax.ShapeDtypeStruct(out_shape, F32),
        grid=(rows_total // tm,),
        in_specs=specs,
        out_specs=out_spec,
        scratch_shapes=[pltpu.VMEM((tm + halo, 2 * fc), F32), pltpu.VMEM((halo, 2 * ff), F32),
                        pltpu.VMEM((tm, ff), BF16)],
        compiler_params=_params(), name="conv_ffn",
    )(*args)


def _norm_proj_kernel(h_ref, g_ref, w_ref, *o_refs, scale):
    hb = _rms(h_ref[...], g_ref[...]).astype(BF16)
    z = jnp.dot(hb, w_ref[...], preferred_element_type=F32)
    col = 0
    for o_ref in o_refs:
        n = o_ref.shape[1]
        o_ref[...] = (z[:, col:col + n] * scale).astype(o_ref.dtype)
        col += n


def _norm_proj(h, g, w, out_widths, out_dtypes, *, tm, scale=1.0):
    rows_total, dm = h.shape
    return pl.pallas_call(
        functools.partial(_norm_proj_kernel, scale=scale),
        out_shape=tuple(jax.ShapeDtypeStruct((rows_total, n), dt) for n, dt in zip(out_widths, out_dtypes)),
        grid=(rows_total // tm,),
        in_specs=[pl.BlockSpec((tm, dm), lambda i: (i, 0)), _const_spec((1, dm)), _const_spec(w.shape)],
        out_specs=tuple(pl.BlockSpec((tm, n), lambda i: (i, 0)) for n in out_widths),
        compiler_params=_params(), name="norm_proj",
    )(h, g, w)


def _kv_proj_kernel(h_ref, g_ref, w_ref, gq_ref, wq_ref, k_ref, vt_ref, f_ref, q_ref, *, q_scale):
    dm = k_ref.shape[1]
    x = h_ref[...]
    hq = _rms(x, gq_ref[...]).astype(BF16)
    q_ref[...] = (jnp.dot(hq, wq_ref[...], preferred_element_type=F32) * q_scale).astype(q_ref.dtype)
    hb = _rms(x, g_ref[...]).astype(BF16)
    z = jnp.dot(hb, w_ref[...], preferred_element_type=F32)
    k_ref[...] = z[:, :dm].astype(k_ref.dtype)
    for p in range(vt_ref.shape[1]):
        lanes = slice(dm + p * V7X_LANES, dm + (p + 1) * V7X_LANES)
        vt_ref[0, p, 0] = z[:, lanes].T.astype(vt_ref.dtype)
    f_ref[...] = z[:, 2 * dm:]


def _kv_proj(h, g, w, g_q, w_q, *, bsz, tm, q_scale):
    rows_total, dm = h.shape
    npair = dm // V7X_LANES
    nk = rows_total // bsz // tm
    fpad = w.shape[1] - 2 * dm
    row_spec = lambda n: pl.BlockSpec((tm, n), lambda i: (i, 0))
    return pl.pallas_call(
        functools.partial(_kv_proj_kernel, q_scale=q_scale),
        out_shape=(jax.ShapeDtypeStruct((rows_total, dm), BF16),
                   jax.ShapeDtypeStruct((bsz, npair, nk, V7X_LANES, tm), BF16),
                   jax.ShapeDtypeStruct((rows_total, fpad), F32),
                   jax.ShapeDtypeStruct((rows_total, dm), BF16)),
        grid=(rows_total // tm,),
        in_specs=[row_spec(dm), _const_spec((1, dm)), _const_spec(w.shape),
                  _const_spec((1, dm)), _const_spec(w_q.shape)],
        out_specs=(row_spec(dm),
                   pl.BlockSpec((1, npair, 1, V7X_LANES, tm), lambda i: (i // nk, 0, i % nk, 0, 0)),
                   row_spec(fpad), row_spec(dm)),
        compiler_params=_params(), name="kv_proj",
    )(h, g, w, g_q, w_q)


LOG2E = 1.4426950408889634
N_SPLIT = 3


def _cumsum_kernel(f_ref, b_ref, c_ref, *part_refs):
    x = jax.nn.log_sigmoid(f_ref[...] + b_ref[...])
    n = x.shape[1]
    lane = lax.broadcasted_iota(jnp.int32, x.shape, 1)
    shift = 1
    while shift < n:
        x = x + jnp.where(lane >= shift, pltpu.roll(x, shift, axis=1), 0.0)
        shift *= 2
    c = x * LOG2E
    c_ref[...] = c
    rest = -c
    for ref in part_refs:
        piece = rest.astype(BF16)
        ref[...] = piece
        rest = rest - piece.astype(F32)


def _forget_cumsum(f_t, b_col):
    return pl.pallas_call(
        _cumsum_kernel,
        out_shape=(jax.ShapeDtypeStruct(f_t.shape, F32),) + (jax.ShapeDtypeStruct(f_t.shape, BF16),) * N_SPLIT,
        name="forget_cumsum")(f_t, b_col)


ONES_ROWS = 16
PAIRS_PER_STEP = 2


def _attn_kernel(q_ref, k_ref, kaug_ref, vt_ref, cq_ref, o_ref, z0_ref, z1_ref, acc_ref, ws_ref, *, hd):
    npp, nq, tq = cq_ref.shape[1], cq_ref.shape[2], cq_ref.shape[4]
    tk = vt_ref.shape[4]
    heads = tuple((pp, h2) for pp in range(npp) for h2 in range(2))
    keep = (lax.broadcasted_iota(jnp.int32, (tk, tq), 0)
            <= lax.broadcasted_iota(jnp.int32, (tk, tq), 1))

    def tile_rows(t):
        return pl.ds(pl.multiple_of(t * tq, tq), tq)

    def pair_lanes(pp):
        return slice(pp * V7X_LANES, (pp + 1) * V7X_LANES)

    def prep_tile(t):
        for pp in range(npp):
            qt = q_ref[0, tile_rows(t), pair_lanes(pp)].astype(F32).T
            row = lax.broadcasted_iota(jnp.int32, qt.shape, 0)
            for h2 in range(2):
                qm = jnp.where((row >= h2 * hd) & (row < (h2 + 1) * hd), qt, 0.0)
                aw = jnp.where((row >= N_SPLIT * h2) & (row < N_SPLIT * (h2 + 1)), 1.0, 0.0)
                ws_ref[2 * pp + h2] = jnp.concatenate([qm, aw], axis=0).astype(BF16)

    def key_rows(j):
        rows = pl.ds(pl.multiple_of(j * tk, tk), tk)
        return tuple(jnp.concatenate([k_ref[0, rows, pair_lanes(pp)], kaug_ref[0, pp, rows, :]], axis=1)
                     for pp in range(npp))

    def scores_head(z_ref, kx, c):
        z = jnp.dot(kx[heads[c][0]], ws_ref[c], preferred_element_type=F32)
        z_ref[c] = z
        return jnp.max(z, axis=0, keepdims=True)

    def consume_head(z_ref, j, t, c, zm, m, masked):
        pp, h2 = heads[c]
        z, c_row = z_ref[c], cq_ref[0, pp, t, h2:h2 + 1, :]
        if masked:
            z = jnp.where(keep, z, -jnp.inf)
            zm = jnp.max(z, axis=0, keepdims=True)
        m_new = jnp.maximum(m, zm + c_row)
        alpha = jnp.exp2(m - m_new)
        p = jnp.exp2(z - (m_new - c_row)).astype(BF16)
        v_ext = jnp.concatenate([vt_ref[0, pp, j, h2 * hd:(h2 + 1) * hd, :],
                                 jnp.ones((ONES_ROWS, tk), BF16)], axis=0)
        acc_ref[c] = alpha * acc_ref[c] + jnp.dot(v_ext, p, preferred_element_type=F32)
        return m_new

    zbufs = (z0_ref, z1_ref)
    fresh_stats = tuple(jnp.full((1, tq), NEG_BIG, F32) for _ in heads)

    def inner_body(t, j, zmax, stats, pending):
        kx = key_rows(j + 1)
        new_zmax, new_stats = [], []
        for c in range(len(heads)):
            new_zmax.append(scores_head(zbufs[1 - pending], kx, c))
            new_stats.append(consume_head(zbufs[pending], j, t, c, zmax[c], stats[c], False))
        return tuple(new_zmax), tuple(new_stats)

    def last_body(t, stats, pending):
        prep_tile(jnp.minimum(t + 1, nq - 1))
        kx = key_rows(0)
        new_zmax = []
        for c in range(len(heads)):
            new_zmax.append(scores_head(zbufs[1 - pending], kx, c))
            consume_head(zbufs[pending], t, t, c, None, stats[c], True)
        for pp in range(npp):
            out_t = jnp.concatenate([acc_ref[2 * pp + h2, 0:hd, :] / acc_ref[2 * pp + h2, hd:hd + 1, :]
                                     for h2 in range(2)], axis=0)
            o_ref[0, tile_rows(t), pair_lanes(pp)] = out_t.T.astype(o_ref.dtype)
        acc_ref[...] = jnp.zeros_like(acc_ref)
        return tuple(new_zmax)

    def by_parity(fn, pending, *args):
        return lax.cond(pending == 0, functools.partial(fn, pending=0), functools.partial(fn, pending=1), *args)

    def tile_body(t, carry):
        zmax, pending = carry

        def step(j, inner):
            zmax, stats, pending = inner
            zmax, stats = by_parity(inner_body, pending, t, j, zmax, stats)
            return zmax, stats, 1 - pending

        zmax, stats, pending = lax.fori_loop(0, t, step, (zmax, fresh_stats, pending))
        return by_parity(last_body, pending, t, stats), 1 - pending

    prep_tile(0)
    acc_ref[...] = jnp.zeros_like(acc_ref)
    kx0 = key_rows(0)
    zmax0 = tuple(scores_head(z0_ref, kx0, c) for c in range(len(heads)))
    lax.fori_loop(0, nq, tile_body, (zmax0, jnp.int32(0)))


def _attention(q, k, kaug, vt, cq, *, hd):
    bsz, seq, dm = q.shape
    npair, nk, tq = vt.shape[1], vt.shape[2], vt.shape[4]
    npp = PAIRS_PER_STEP if npair % PAIRS_PER_STEP == 0 else 1
    nch = 2 * npp
    lane_spec = pl.BlockSpec((1, seq, npp * V7X_LANES), lambda b, p: (b, 0, p))
    return pl.pallas_call(
        functools.partial(_attn_kernel, hd=hd),
        out_shape=jax.ShapeDtypeStruct(q.shape, BF16),
        grid=(bsz, npair // npp),
        in_specs=[lane_spec, lane_spec,
                  pl.BlockSpec((1, npp, seq, V7X_LANES), lambda b, p: (b, p, 0, 0)),
                  pl.BlockSpec((1, npp, nk, V7X_LANES, tq), lambda b, p: (b, p, 0, 0, 0)),
                  pl.BlockSpec((1, npp, nk, 2, tq), lambda b, p: (b, p, 0, 0, 0))],
        out_specs=lane_spec,
        scratch_shapes=[pltpu.VMEM((nch, tq, tq), F32), pltpu.VMEM((nch, tq, tq), F32),
                        pltpu.VMEM((nch, hd + ONES_ROWS, tq), F32),
                        pltpu.VMEM((nch, 2 * V7X_LANES, tq), BF16)],
        compiler_params=_params(2), name="fox_attention",
    )(q, k, kaug, vt, cq)


def kernel(x, g_mix, g_ffn, lam_re, lam_im, log_dt, ssm_b_re, ssm_b_im, ssm_c_re, ssm_c_im, ssm_d,
           w_glu, g_kv, w_kvf, b_f, w_q, w_o, w_ffn_in, ffn_conv_w, ffn_conv_b, w_ffn_out, g_final):
    bsz, seq, dm = x.shape
    depth = g_mix.shape[0]
    n_a = lam_re.shape[0]
    nh = b_f.shape[0]
    hd = dm // nh
    ff = w_ffn_out.shape[1]
    assert bsz == V7X_SUBLANES and 2 * hd == V7X_LANES

    tm = min(512, seq)
    steps = tm // bsz
    fc = 256 if ff % 256 == 0 else V7X_LANES
    tq = min(512, seq)
    row = lambda a: a.reshape(1, -1)

    w_in_b = w_ffn_in.astype(BF16)
    w_out_b = w_ffn_out.astype(BF16)

    def ffn(h, layer, shift, seq_tiles, **fused):
        return _ffn_layer(h, row(g_ffn[layer]), w_in_b[layer], ffn_conv_w[layer], row(ffn_conv_b[layer]),
                          w_out_b[layer], tm=tm, shift=shift, seq_tiles=seq_tiles, fc=fc, **fused)

    h = x
    for layer in range(n_a):
        lb_re, lb_im, bb_re, bb_im = _s5_prep(lam_re[layer], lam_im[layer], log_dt[layer],
                                              ssm_b_re[layer], ssm_b_im[layer])
        h = _s5_layer(h, row(g_mix[layer]), _block_diag_in(bb_re, bb_im).astype(BF16),
                      _block_diag_out(ssm_c_re[layer]).astype(BF16),
                      _block_diag_out(ssm_c_im[layer]).astype(BF16),
                      row(lb_re), row(lb_im), row(ssm_d[layer]), w_glu[layer].astype(BF16),
                      bsz=bsz, steps=steps)
        h = ffn(h, layer, bsz, seq * bsz // tm, batch_major_out=bsz if layer == n_a - 1 else None)

    h = h.reshape(bsz * seq, dm)
    fpad = V7X_LANES
    w_kvf_b = jnp.pad(w_kvf, ((0, 0), (0, fpad - nh))).astype(BF16)
    q_scale = hd ** -0.5 * LOG2E
    k, vt, f_logit, q_first = _kv_proj(h, row(g_kv), w_kvf_b, row(g_mix[n_a]), w_q[0].astype(BF16),
                                       bsz=bsz, tm=tq, q_scale=q_scale)
    f_t = f_logit[:, :nh].reshape(bsz, seq, nh).transpose(0, 2, 1).reshape(bsz * nh, seq)
    cum_t, *parts = _forget_cumsum(f_t, jnp.tile(b_f, bsz).reshape(bsz * nh, 1))
    npair = nh // 2
    nk = seq // tq
    cq = cum_t.reshape(bsz, npair, 2, nk, tq).transpose(0, 1, 3, 2, 4)
    kaug = jnp.stack(parts, axis=1).reshape(bsz, npair, 2 * N_SPLIT, seq).transpose(0, 1, 3, 2)
    kaug = jnp.pad(kaug, ((0, 0), (0, 0), (0, 0), (0, V7X_LANES - 2 * N_SPLIT)))
    k3 = k.reshape(bsz, seq, dm)
    for layer in range(n_a, depth):
        j = layer - n_a
        if j == 0:
            q = q_first
        else:
            (q,) = _norm_proj(h, row(g_mix[layer]), w_q[j].astype(BF16), (dm,), (BF16,), tm=tm,
                              scale=q_scale)
        o = _attention(q.reshape(bsz, seq, dm), k3, kaug, vt, cq, hd=hd)
        h = ffn(h, layer, 1, seq // tm, attn_out=o.reshape(bsz * seq, dm), w_o=w_o[j].astype(BF16),
                g_final=row(g_final) if layer == depth - 1 else None)
    return h.reshape(bsz, seq, dm)
```
